```python
import jax, jax.numpy as jnp
from jax import lax
import numpy as np

D_MODEL = 1024
BATCH = 8
SEQ = 2048
DEPTH = 2
DEC_BATCH = 128
DEC_SEQ = 1
PAST_LEN = 8192
PAGE_SIZE = 128

HEAD_DIM = 64
N_A_LAYERS = DEPTH // 2
N_B_LAYERS = DEPTH - N_A_LAYERS
H_A = D_MODEL // HEAD_DIM
G_A = 4
R_A = H_A // G_A
CMP_LEN = 32
CMP_STRIDE = 16
CMP_RATIO = CMP_LEN // CMP_STRIDE
CMP_HID = 128
SEL_BLOCK = 64
SEL_PER_PAGE = PAGE_SIZE // SEL_BLOCK
N_SEL = 8
SEL_Q_BLOCK = 64
WIN_A = 512
NSA_Q = H_A * HEAD_DIM
NSA_KV = 2 * G_A * HEAD_DIM
NSA_IN = NSA_Q + 3 * NSA_KV + 3 * H_A
H_B = D_MODEL // HEAD_DIM
G_B = 2
R_B = H_B // G_B
WIN_B = 128
BAND_BLOCK = 128
PEER_HEADS = 8
PEER_QDIM = 256
N_KEYS = 128
N_EXPERTS = N_KEYS * N_KEYS
PEER_TOPK = 16
PEER_BLOCK = 256
RMS_EPS = 1e-6
NEG = -1e30
FORCE = 1e3

kernel_name = 'yoco_nsa_swa_sink_peer_step'


def rmsnorm(x, g):
    xf = x.astype(jnp.float32)
    r = lax.rsqrt(jnp.mean(xf * xf, axis=-1, keepdims=True) + RMS_EPS)
    return (xf * r).astype(x.dtype) * g


def alibi_slopes(n_heads, n_groups):
    s = 2.0 ** (-8.0 * jnp.arange(1, n_heads + 1, dtype=jnp.float32) / n_heads)
    return s.reshape(n_groups, n_heads // n_groups)


def local_attend(q, k, v, qpos, kpos, window, slopes, sinks):
    s = jnp.einsum('bntgrd,bnkgd->bngrtk', q, k).astype(jnp.float32) * (HEAD_DIM ** -0.5)
    dist = qpos[:, :, None] - kpos[:, None, :]
    valid = (dist >= 0) & (dist < window) & (kpos[:, None, :] >= 0)
    s = s - slopes[:, :, None, None] * dist[None, :, None, None].astype(jnp.float32)
    s = jnp.where(valid[None, :, None, None], s, NEG)
    if sinks is not None:
        sk = jnp.broadcast_to(sinks.astype(jnp.float32)[:, :, None, None], s.shape[:-1] + (1,))
        p = jax.nn.softmax(jnp.concatenate([s, sk], axis=-1), axis=-1)[..., :-1]
    else:
        p = jax.nn.softmax(s, axis=-1)
    return jnp.einsum('bngrtk,bnkgd->bntgrd', p.astype(v.dtype), v)


def banded_prompt(q, kv, window, slopes, sinks):
    B, S, G, R, dh = q.shape
    nb = S // BAND_BLOCK
    nprev = window // BAND_BLOCK
    kvb = jnp.pad(kv.reshape(B, nb, BAND_BLOCK, 2, G, dh), ((0, 0), (nprev, 0), (0, 0), (0, 0), (0, 0), (0, 0)))
    kk = jnp.concatenate([kvb[:, j:j + nb] for j in range(nprev + 1)], axis=2)
    qpos = jnp.arange(S).reshape(nb, BAND_BLOCK)
    kpos = (jnp.arange(nb)[:, None] - nprev) * BAND_BLOCK + jnp.arange((nprev + 1) * BAND_BLOCK)[None, :]
    o = local_attend(q.reshape(B, nb, BAND_BLOCK, G, R, dh), kk[:, :, :, 0], kk[:, :, :, 1], qpos, kpos, window, slopes, sinks)
    return o.reshape(B, S, G, R, dh)


def cmp_chunk_proj(kv, w1):
    B, L, _, G, dh = kv.shape
    ch = kv.reshape(B, L // CMP_STRIDE, CMP_STRIDE, 2, G, dh)
    w1r = w1.reshape(2, CMP_RATIO, CMP_STRIDE, dh, CMP_HID)
    return jnp.einsum('bcskgd,krsdh->bcrkgh', ch, w1r)


def cmp_finish(p, pos_c, w1, b1, w2):
    nb = p.shape[1] - CMP_RATIO + 1
    acc = p[:, :nb, 0]
    for r in range(1, CMP_RATIO):
        acc = acc + p[:, r:r + nb, r]
    bias = b1 + jnp.einsum('kf,kfh->kh', pos_c.reshape(2, -1), w1)
    hid = jax.nn.gelu(acc + bias[:, None, :])
    return jnp.einsum('bnkgh,khd->bnkgd', hid, w2)


def cmp_attend(q, kvc, qpos, slopes):
    nb = kvc.shape[1]
    blk_end = jnp.arange(nb) * CMP_STRIDE + CMP_LEN - 1
    s = jnp.einsum('btgrd,bngd->bgrtn', q, kvc[:, :, 0]).astype(jnp.float32) * (HEAD_DIM ** -0.5)
    dist = qpos[:, None] - blk_end[None, :]
    valid = dist >= 0
    s = s - slopes[:, :, None, None] * dist.astype(jnp.float32)
    p = jax.nn.softmax(jnp.where(valid, s, NEG), axis=-1) * jnp.any(valid, axis=-1)[:, None].astype(jnp.float32)
    o = jnp.einsum('bgrtn,bngd->btgrd', p.astype(kvc.dtype), kvc[:, :, 1])
    return o, p


def select_blocks(p, qpos, seq_len):
    nb = p.shape[-1]
    ns = -(-seq_len // SEL_BLOCK)
    cs = jnp.arange(nb) * CMP_STRIDE
    ss = jnp.arange(ns) * SEL_BLOCK
    ov = jnp.maximum(jnp.minimum(cs[:, None] + CMP_LEN, ss[None, :] + SEL_BLOCK) - jnp.maximum(cs[:, None], ss[None, :]), 0)
    ov = ov.astype(jnp.float32) / CMP_LEN
    imp = jnp.einsum('bgrtn,nj->bgtj', p, ov)
    j = jnp.arange(ns)[None, :]
    cur = (qpos // SEL_BLOCK)[:, None]
    forced = (j == 0) | (j == cur) | (j == cur - 1)
    valid = j * SEL_BLOCK <= qpos[:, None]
    score = jnp.where(valid, imp + FORCE * forced.astype(jnp.float32), NEG)
    _, idx = lax.top_k(score, min(N_SEL, ns))
    return idx


def sel_attend(q, kv, kpos, qpos, slopes):
    B, G, T, K, L = kpos.shape
    kv = kv.reshape(B, G, T, K * L, 2, kv.shape[-1])
    kpos = kpos.reshape(B, G, T, K * L)
    s = jnp.einsum('btgrd,bgtkd->bgrtk', q, kv[..., 0, :]).astype(jnp.float32) * (HEAD_DIM ** -0.5)
    dist = qpos[None, None, :, None] - kpos
    s = s - slopes[None, :, :, None, None] * dist[:, :, None].astype(jnp.float32)
    p = jax.nn.softmax(jnp.where((dist >= 0)[:, :, None], s, NEG), axis=-1)
    return jnp.einsum('bgrtk,bgtkd->btgrd', p.astype(kv.dtype), kv[..., 1, :])


def sel_prompt(q, kv_sel, idx, slopes):
    B, S, G, R, dh = q.shape
    ns = S // SEL_BLOCK
    kvb = kv_sel.reshape(B, ns, SEL_BLOCK, 2, G, dh).transpose(0, 4, 1, 2, 3, 5)
    nq = S // SEL_Q_BLOCK
    qb = q.reshape(B, nq, SEL_Q_BLOCK, G, R, dh).swapaxes(0, 1)
    ib = idx.reshape(B, G, nq, SEL_Q_BLOCK, idx.shape[-1]).transpose(2, 0, 1, 3, 4)
    pb = jnp.arange(S).reshape(nq, SEL_Q_BLOCK)
    bi = jnp.arange(B)[:, None, None, None]
    gi = jnp.arange(G)[None, :, None, None]

    def one(args):
        qq, ii, pp = args
        kv = kvb[bi, gi, ii]
        kpos = ii[..., None] * SEL_BLOCK + jnp.arange(SEL_BLOCK)
        return sel_attend(qq, kv, kpos, pp, slopes)

    o = lax.map(one, (qb, ib, pb))
    return o.swapaxes(0, 1).reshape(B, S, G, R, dh)


def sel_gather_sample(pool, la, page_table, kv_new, idx):
    Bd, G, T, K = idx.shape
    npb = page_table.shape[1] * SEL_PER_PAGE
    bi = jnp.arange(Bd)[:, None, None, None]
    gi = jnp.arange(G)[None, :, None, None]
    jp = jnp.minimum(idx, npb - 1)
    phys = page_table[bi, jp // SEL_PER_PAGE]
    rows = (jp % SEL_PER_PAGE)[..., None] * SEL_BLOCK + jnp.arange(SEL_BLOCK)
    past = pool[la, phys[..., None], rows, :, gi[..., None]]
    tn = kv_new.shape[1]
    nnb = -(-tn // SEL_BLOCK)
    newb = jnp.pad(kv_new, ((0, 0), (0, nnb * SEL_BLOCK - tn), (0, 0), (0, 0), (0, 0)))
    newb = newb.reshape(Bd, nnb, SEL_BLOCK, 2, G, kv_new.shape[-1])
    jn = jnp.clip(idx - npb, 0, nnb - 1)
    new = newb[bi, jn, :, :, gi]
    return jnp.where((idx >= npb)[..., None, None, None], new, past)


def nsa_split(h, w_in):
    B, T, _ = h.shape
    z = h @ w_in
    q = z[..., :NSA_Q].reshape(B, T, G_A, R_A, HEAD_DIM)
    o1 = NSA_Q
    kv_c = z[..., o1:o1 + NSA_KV].reshape(B, T, 2, G_A, HEAD_DIM)
    kv_s = z[..., o1 + NSA_KV:o1 + 2 * NSA_KV].reshape(B, T, 2, G_A, HEAD_DIM)
    kv_w = z[..., o1 + 2 * NSA_KV:o1 + 3 * NSA_KV].reshape(B, T, 2, G_A, HEAD_DIM)
    gates = jax.nn.sigmoid(z[..., o1 + 3 * NSA_KV:]).reshape(B, T, 3, G_A, R_A)
    return q, kv_c, kv_s, kv_w, gates


def nsa_merge(gates, o_c, o_s, o_w, w_o):
    B, T = o_c.shape[:2]
    o = gates[:, :, 0, :, :, None] * o_c + gates[:, :, 1, :, :, None] * o_s + gates[:, :, 2, :, :, None] * o_w
    return o.reshape(B, T, NSA_Q) @ w_o


def nsa_prompt(h, w_in, w_o, pos_c, w1, b1, w2, slopes):
    q, kv_c, kv_s, kv_w, gates = nsa_split(h, w_in)
    S = h.shape[1]
    qpos = jnp.arange(S)
    pad = (-S) % CMP_STRIDE
    pc = cmp_chunk_proj(jnp.pad(kv_c, ((0, 0), (0, pad), (0, 0), (0, 0), (0, 0))), w1)
    kvc = cmp_finish(pc, pos_c, w1, b1, w2)
    o_c, p_c = cmp_attend(q, kvc, qpos, slopes)
    idx = select_blocks(p_c, qpos, S)
    o_s = sel_prompt(q, kv_s, idx, slopes)
    o_w = banded_prompt(q, kv_w, WIN_A, slopes, None)
    y = nsa_merge(gates, o_c, o_s, o_w, w_o)
    nw = min(WIN_A, S)
    return y, (kv_c, kv_s, kv_w[:, S - nw:])


def nsa_sample(h, cache_cmp, cache_sel, win_buf, page_table, la, w_in, w_o, pos_c, w1, b1, w2, slopes):
    q, kv_c, kv_s, kv_w, gates = nsa_split(h, w_in)
    Bd, T = h.shape[:2]
    past_len = page_table.shape[1] * PAGE_SIZE
    qpos = past_len + jnp.arange(T)
    past_c = cache_cmp[la, page_table].reshape(Bd, past_len, 2, G_A, HEAD_DIM)
    pad = (-T) % CMP_STRIDE
    pc = jnp.concatenate([cmp_chunk_proj(past_c, w1),
                          cmp_chunk_proj(jnp.pad(kv_c, ((0, 0), (0, pad), (0, 0), (0, 0), (0, 0))), w1)], axis=1)
    kvc = cmp_finish(pc, pos_c, w1, b1, w2)
    o_c, p_c = cmp_attend(q, kvc, qpos, slopes)
    idx = select_blocks(p_c, qpos, past_len + T)
    kv_g = sel_gather_sample(cache_sel, la, page_table, kv_s, idx)
    kpos = idx[..., None] * SEL_BLOCK + jnp.arange(SEL_BLOCK)
    o_s = sel_attend(q, kv_g, kpos, qpos, slopes)
    nbuf = win_buf.shape[1]
    keys = jnp.concatenate([win_buf, kv_w], axis=1)
    kpos_w = past_len - nbuf + jnp.arange(nbuf + T)
    o_w = local_attend(q[:, None], keys[:, None, :, 0], keys[:, None, :, 1], qpos[None], kpos_w[None], WIN_A, slopes, None)[:, 0]
    y = nsa_merge(gates, o_c, o_s, o_w, w_o)
    return y, (kv_c, kv_s, keys[:, T:])


def peer(h, w_pq, keys, u_tab, v_tab):
    B, T, D = h.shape
    n = B * T
    nblk = -(-n // PEER_BLOCK)
    xf = jnp.pad(h.reshape(n, D), ((0, nblk * PEER_BLOCK - n), (0, 0))).reshape(nblk, PEER_BLOCK, D)

    def one(xb):
        qh = (xb @ w_pq).reshape(PEER_BLOCK, PEER_HEADS, 2, PEER_QDIM // 2)
        s = jnp.einsum('thcd,hckd->thck', qh, keys).astype(jnp.float32)
        sv, si = lax.top_k(s, PEER_TOPK)
        cand = (sv[:, :, 0, :, None] + sv[:, :, 1, None, :]).reshape(PEER_BLOCK, PEER_HEADS, PEER_TOPK * PEER_TOPK)
        cid = (si[:, :, 0, :, None] * N_KEYS + si[:, :, 1, None, :]).reshape(PEER_BLOCK, PEER_HEADS, PEER_TOPK * PEER_TOPK)
        ts, tp = lax.top_k(cand, PEER_TOPK)
        eid = jnp.take_along_axis(cid, tp, axis=-1)
        g = jax.nn.softmax(ts, axis=-1)
        a = jax.nn.gelu(jnp.einsum('thkd,td->thk', u_tab[eid], xb).astype(jnp.float32))
        return jnp.einsum('thk,thkd->td', (g * a).astype(xb.dtype), v_tab[eid])

    return lax.map(one, xf).reshape(nblk * PEER_BLOCK, D)[:n].reshape(B, T, D)


def trunk(x, c, mixer_a, mixer_b, w_ada, b_ada, g_norm, g_kv, w_ada_kv, b_ada_kv, w_kv_b, w_pq, peer_keys, peer_u, peer_v, g_final):
    B, T, _ = x.shape
    cs = jax.nn.silu(c)
    states = []
    kv_sh = None
    for i in range(DEPTH):
        sh1, sc1, ga1, sh2, sc2, ga2 = [m[:, None, :] for m in jnp.split(cs @ w_ada[i] + b_ada[i], 6, axis=-1)]
        if i == N_A_LAYERS:
            shk, sck = [m[:, None, :] for m in jnp.split(cs @ w_ada_kv + b_ada_kv, 2, axis=-1)]
            hk = rmsnorm(x, g_kv) * (1 + sck) + shk
            kv_sh = (hk @ w_kv_b).reshape(B, T, 2, G_B, HEAD_DIM)
        h = rmsnorm(x, g_norm[i, 0]) * (1 + sc1) + sh1
        if i < N_A_LAYERS:
            o, st = mixer_a(h, i)
            states.append(st)
        else:
            o = mixer_b(h, i - N_A_LAYERS, kv_sh)
        x = x + ga1 * o
        h = rmsnorm(x, g_norm[i, 1]) * (1 + sc2) + sh2
        x = x + ga2 * peer(h, w_pq[i], peer_keys[i], peer_u[i], peer_v[i])
    return rmsnorm(x, g_final), states, kv_sh


def setup_inputs(seed: int = 0) -> dict:
    key = jax.random.key(seed)
    ks = list(jax.random.split(key, 32))

    def nrm(i, shape, scale=1.0):
        return jax.random.normal(ks[i], shape, jnp.float32) * scale

    D = D_MODEL
    n_pages = PAST_LEN // PAGE_SIZE
    n_phys = (DEC_BATCH * n_pages * 5) // 4
    page_table = jax.random.permutation(ks[0], n_phys)[:DEC_BATCH * n_pages].reshape(DEC_BATCH, n_pages).astype(jnp.int32)
    return {
        'x_prompt': nrm(1, (BATCH, SEQ, D)),
        'x_sample': nrm(2, (DEC_BATCH, DEC_SEQ, D)),
        'c_prompt': nrm(3, (BATCH, D)),
        'c_sample': nrm(4, (DEC_BATCH, D)),
        'cache_cmp_kv': nrm(5, (N_A_LAYERS, n_phys, PAGE_SIZE, 2, G_A, HEAD_DIM)),
        'cache_sel_kv': nrm(6, (N_A_LAYERS, n_phys, PAGE_SIZE, 2, G_A, HEAD_DIM)),
        'state_win_kv': nrm(7, (N_A_LAYERS, DEC_BATCH, min(WIN_A, PAST_LEN), 2, G_A, HEAD_DIM)),
        'state_shared_kv': nrm(8, (DEC_BATCH, min(WIN_B, PAST_LEN), 2, G_B, HEAD_DIM)),
        'page_table': page_table,
        'w_ada': nrm(9, (DEPTH, D, 6 * D), 0.5 * D ** -0.5),
        'b_ada': nrm(10, (DEPTH, 6 * D), 0.01),
        'g_norm': 1.0 + nrm(11, (DEPTH, 2, D), 0.01),
        'w_in_a': nrm(12, (N_A_LAYERS, D, NSA_IN), D ** -0.5),
        'w_o_a': nrm(13, (N_A_LAYERS, NSA_Q, D), NSA_Q ** -0.5),
        'cmp_pos': nrm(14, (N_A_LAYERS, 2, CMP_LEN, HEAD_DIM), 0.1),
        'w_cmp1': nrm(15, (N_A_LAYERS, 2, CMP_LEN * HEAD_DIM, CMP_HID), (CMP_LEN * HEAD_DIM) ** -0.5),
        'b_cmp1': nrm(16, (N_A_LAYERS, 2, CMP_HID), 0.01),
        'w_cmp2': nrm(17, (N_A_LAYERS, 2, CMP_HID, HEAD_DIM), CMP_HID ** -0.5),
        'g_kv': 1.0 + nrm(18, (D,), 0.01),
        'w_ada_kv': nrm(19, (D, 2 * D), 0.5 * D ** -0.5),
        'b_ada_kv': nrm(20, (2 * D,), 0.01),
        'w_kv_b': nrm(21, (D, 2 * G_B * HEAD_DIM), D ** -0.5),
        'w_q_b': nrm(22, (N_B_LAYERS, D, H_B * HEAD_DIM), D ** -0.5),
        'w_o_b': nrm(23, (N_B_LAYERS, H_B * HEAD_DIM, D), (H_B * HEAD_DIM) ** -0.5),
        'sinks': nrm(24, (N_B_LAYERS, H_B), 0.5),
        'w_pq': nrm(25, (DEPTH, D, PEER_HEADS * PEER_QDIM), D ** -0.5),
        'peer_keys': nrm(26, (DEPTH, PEER_HEADS, 2, N_KEYS, PEER_QDIM // 2), (PEER_QDIM // 2) ** -0.5),
        'peer_u': nrm(27, (DEPTH, N_EXPERTS, D), D ** -0.5),
        'peer_v': nrm(28, (DEPTH, N_EXPERTS, D), PEER_HEADS ** -0.5),
        'g_final': 1.0 + nrm(29, (D,), 0.01),
    }


def reference(x_prompt, x_sample, c_prompt, c_sample, cache_cmp_kv, cache_sel_kv, state_win_kv, state_shared_kv, page_table,
              w_ada, b_ada, g_norm, w_in_a, w_o_a, cmp_pos, w_cmp1, b_cmp1, w_cmp2, g_kv, w_ada_kv, b_ada_kv, w_kv_b,
              w_q_b, w_o_b, sinks, w_pq, peer_keys, peer_u, peer_v, g_final):
    slopes_a = alibi_slopes(H_A, G_A)
    slopes_b = alibi_slopes(H_B, G_B)
    past_len = page_table.shape[1] * PAGE_SIZE
    S = x_prompt.shape[1]
    T = x_sample.shape[1]

    def mixer_a_prompt(h, la):
        return nsa_prompt(h, w_in_a[la], w_o_a[la], cmp_pos[la], w_cmp1[la], b_cmp1[la], w_cmp2[la], slopes_a)

    def mixer_a_sample(h, la):
        return nsa_sample(h, cache_cmp_kv, cache_sel_kv, state_win_kv[la], page_table, la,
                          w_in_a[la], w_o_a[la], cmp_pos[la], w_cmp1[la], b_cmp1[la], w_cmp2[la], slopes_a)

    def mixer_b_prompt(h, lb, kv_sh):
        B = h.shape[0]
        q = (h @ w_q_b[lb]).reshape(B, S, G_B, R_B, HEAD_DIM)
        o = banded_prompt(q, kv_sh, WIN_B, slopes_b, sinks[lb].reshape(G_B, R_B))
        return o.reshape(B, S, H_B * HEAD_DIM) @ w_o_b[lb]

    def mixer_b_sample(h, lb, kv_sh):
        Bd = h.shape[0]
        q = (h @ w_q_b[lb]).reshape(Bd, T, G_B, R_B, HEAD_DIM)
        nbuf = state_shared_kv.shape[1]
        keys = jnp.concatenate([state_shared_kv, kv_sh], axis=1)
        qpos = past_len + jnp.arange(T)
        kpos = past_len - nbuf + jnp.arange(nbuf + T)
        o = local_attend(q[:, None], keys[:, None, :, 0], keys[:, None, :, 1], qpos[None], kpos[None], WIN_B, slopes_b,
                         sinks[lb].reshape(G_B, R_B))[:, 0]
        return o.reshape(Bd, T, H_B * HEAD_DIM) @ w_o_b[lb]

    shared = (w_ada, b_ada, g_norm, g_kv, w_ada_kv, b_ada_kv, w_kv_b, w_pq, peer_keys, peer_u, peer_v, g_final)
    y_prompt, st_p, kvsh_p = trunk(x_prompt, c_prompt, mixer_a_prompt, mixer_b_prompt, *shared)
    y_sample, st_s, kvsh_s = trunk(x_sample, c_sample, mixer_a_sample, mixer_b_sample, *shared)

    new_cmp_prompt = jnp.stack([s[0] for s in st_p])
    new_sel_prompt = jnp.stack([s[1] for s in st_p])
    new_win_prompt = jnp.stack([s[2] for s in st_p])
    new_cmp_sample = jnp.stack([s[0] for s in st_s])
    new_sel_sample = jnp.stack([s[1] for s in st_s])
    new_win_sample = jnp.stack([s[2] for s in st_s])
    nwb = min(WIN_B, S)
    new_shared_prompt = kvsh_p[:, S - nwb:]
    new_shared_sample = jnp.concatenate([state_shared_kv, kvsh_s], axis=1)[:, T:]
    return (y_prompt, y_sample, new_cmp_prompt, new_cmp_sample, new_sel_prompt, new_sel_sample,
            new_win_prompt, new_win_sample, new_shared_prompt, new_shared_sample)
```

```python
import functools

import jax
import jax.numpy as jnp
from jax import lax
from jax.experimental import pallas as pl
from jax.experimental.pallas import tpu as pltpu

F32 = jnp.float32
BF16 = jnp.bfloat16

D_MODEL = 1024
HEAD_DIM = 64
PAGE_SIZE = 128
H_A = 16
G_A = 4
R_A = H_A // G_A
CMP_LEN = 32
CMP_STRIDE = 16
CMP_HID = 128
SEL_BLOCK = 64
N_SEL = 8
WIN_A = 512
NSA_Q = H_A * HEAD_DIM
NSA_KV = 2 * G_A * HEAD_DIM
NSA_IN = NSA_Q + 3 * NSA_KV + 3 * H_A
H_B = 16
G_B = 2
R_B = H_B // G_B
WIN_B = 128
PEER_HEADS = 8
PEER_QDIM = 256
N_KEYS = 128
PEER_TOPK = 16
RMS_EPS = 1e-6
NEG = -1e30
FORCE = 1e3

LANES = 128
SUBLANES = 8
VMEM_LIMIT = 56 * 1024 * 1024


def _cparams(sem):
    return pltpu.CompilerParams(dimension_semantics=sem, vmem_limit_bytes=VMEM_LIMIT)


def _gelu(x):
    return x * (0.5 * (1.0 + jnp.tanh(0.7978845608028654 * (x + 0.044715 * (x * x * x)))))


def _normmod(x, g, sc, sh):
    r = lax.rsqrt(jnp.mean(x * x, axis=-1, keepdims=True) + RMS_EPS)
    return (x * r) * g * (1.0 + sc) + sh


N_TOPV = PEER_TOPK + 1
_PAIRS = [(p, q) for p in range(N_TOPV) for q in range(N_TOPV) if (p + 1) * (q + 1) <= N_TOPV]


def _peer_prep_kernel(x_ref, sc_ref, sh_ref, g_ref, wpqT_ref, keys_ref,
                      hT_ref, s1_ref, b_ref, c_ref, a_ref, qT_ref, topv_ref):
    T = x_ref.shape[1]
    h = _normmod(x_ref[0], g_ref[...], sc_ref[0], sh_ref[0])
    hT = h.T.astype(BF16)
    hT_ref[...] = hT
    qT_ref[...] = jnp.dot(wpqT_ref[...], hT, preferred_element_type=F32).astype(BF16)

    def per_head(hh, carry):
        for c in range(2):
            hc = hh * 2 + c
            q = qT_ref[pl.ds(pl.multiple_of(hc * N_KEYS, N_KEYS), N_KEYS), :]
            s = jnp.dot(keys_ref[hc], q, preferred_element_type=F32)
            if c == 0:
                c_ref[hh] = s
            else:
                s1_ref[hh] = s
            for r in range(N_TOPV):
                m = jnp.max(s, axis=0, keepdims=True)
                topv_ref[c, r, pl.ds(hh, 1), :] = m
                s = jnp.where(s >= m, -jnp.inf, s)
        return carry

    lax.fori_loop(0, PEER_HEADS, per_head, 0)

    for tc in range(T // LANES):
        sl = slice(tc * LANES, (tc + 1) * LANES)
        av = [topv_ref[0, p, :, sl] for p in range(N_TOPV)]
        bv = [topv_ref[1, q, :, sl] for q in range(N_TOPV)]
        cands = [av[p] + bv[q] for (p, q) in _PAIRS]
        top = av[0] + bv[0]
        work = list(cands)
        kth = None
        prev = None
        for r in range(N_TOPV):
            m = work[0]
            for w in work[1:]:
                m = jnp.maximum(m, w)
            prev, kth = kth, m
            work = [jnp.where(w >= m, -jnp.inf, w) for w in work]
        tau = 0.5 * (prev + kth)
        z = jnp.zeros_like(top)
        for cnd in cands:
            z = z + jnp.where(cnd >= tau, jnp.exp(cnd - top), 0.0)
        topv_ref[0, 1, :, sl] = tau
        topv_ref[0, 2, :, sl] = 1.0 / z

    for hh in range(PEER_HEADS):
        a1 = topv_ref[0, 0, hh:hh + 1, :]
        b1 = topv_ref[1, 0, hh:hh + 1, :]
        tau = topv_ref[0, 1, hh:hh + 1, :]
        rz = topv_ref[0, 2, hh:hh + 1, :]
        s0 = c_ref[hh]
        c_ref[hh] = tau - s0
        a_ref[hh] = jnp.exp(s0 - a1) * rz
        b_ref[hh] = jnp.exp(s1_ref[hh] - b1)


def _peer_dense_kernel(hT_ref, s1_ref, b_ref, c_ref, a_ref, u_ref, v_ref, x_ref, ga_ref,
                       o_ref, acc_ref, w_ref, *, n_i):
    i = pl.program_id(2)
    T = hT_ref.shape[1]
    ei = u_ref.shape[0] // N_KEYS

    @pl.when(i == 0)
    def _():
        acc_ref[...] = jnp.zeros_like(acc_ref)

    s = jnp.dot(u_ref[...], hT_ref[...], preferred_element_type=F32)
    gl = _gelu(s)
    for ii in range(ei):
        for tc in range(T // LANES):
            sl = slice(tc * LANES, (tc + 1) * LANES)
            w = jnp.zeros((N_KEYS, LANES), F32)
            for hh in range(PEER_HEADS):
                cth = c_ref[hh, ii:ii + 1, sl]
                ath = a_ref[hh, ii:ii + 1, sl]
                w = w + jnp.where(s1_ref[hh, :, sl] >= cth, b_ref[hh, :, sl], 0.0) * ath
            w_ref[ii * N_KEYS:(ii + 1) * N_KEYS, sl] = (
                w * gl[ii * N_KEYS:(ii + 1) * N_KEYS, sl]).astype(BF16)
    acc_ref[...] += lax.dot_general(w_ref[...], v_ref[...], (((0,), (0,)), ((), ())),
                                    preferred_element_type=F32)

    @pl.when(i == n_i - 1)
    def _():
        o_ref[0] = x_ref[0] + ga_ref[0] * acc_ref[...]


def _mod_spec(per_row, t):
    if per_row:
        return pl.BlockSpec((1, t, D_MODEL), lambda b, j, *_: (0, j, 0))
    return pl.BlockSpec((1, 1, D_MODEL), lambda b, j, *_: (b, 0, 0))


def _peer(x, sc, sh, ga, g, wpqT, keys16, u_bf, v_bf, *, t_tile, ei):
    B, S, D = x.shape
    per_row = sc.shape[1] != 1
    nt = S // t_tile
    ntot = B * S
    n_i = N_KEYS // ei
    col = lambda b, j, *_: (0, b * nt + j)
    col3 = lambda b, j, *_: (0, 0, b * nt + j)
    row3 = lambda b, j, *_: (b, j, 0)
    f = jax.ShapeDtypeStruct
    hT, s1, bb, cc, aa = pl.pallas_call(
        _peer_prep_kernel,
        grid=(B, nt),
        in_specs=[
            pl.BlockSpec((1, t_tile, D), row3),
            _mod_spec(per_row, t_tile), _mod_spec(per_row, t_tile),
            pl.BlockSpec((1, D), lambda b, j: (0, 0)),
            pl.BlockSpec((PEER_HEADS * PEER_QDIM, D), lambda b, j: (0, 0)),
            pl.BlockSpec((2 * PEER_HEADS, N_KEYS, N_KEYS), lambda b, j: (0, 0, 0)),
        ],
        out_specs=[
            pl.BlockSpec((D, t_tile), col),
            pl.BlockSpec((PEER_HEADS, N_KEYS, t_tile), col3),
            pl.BlockSpec((PEER_HEADS, N_KEYS, t_tile), col3),
            pl.BlockSpec((PEER_HEADS, N_KEYS, t_tile), col3),
            pl.BlockSpec((PEER_HEADS, N_KEYS, t_tile), col3),
        ],
        out_shape=[f((D, ntot), BF16)] + [f((PEER_HEADS, N_KEYS, ntot), F32)] * 4,
        scratch_shapes=[pltpu.VMEM((PEER_HEADS * PEER_QDIM, t_tile), BF16),
                        pltpu.VMEM((2, N_TOPV, PEER_HEADS, t_tile), F32)],
        compiler_params=_cparams(("parallel", "parallel")),
        name="peer_prep",
    )(x, sc, sh, g, wpqT, keys16)

    blk = ei * N_KEYS
    return pl.pallas_call(
        functools.partial(_peer_dense_kernel, n_i=n_i),
        grid=(B, nt, n_i),
        in_specs=[
            pl.BlockSpec((D, t_tile), col),
            pl.BlockSpec((PEER_HEADS, N_KEYS, t_tile), col3),
            pl.BlockSpec((PEER_HEADS, N_KEYS, t_tile), col3),
            pl.BlockSpec((PEER_HEADS, ei, t_tile), lambda b, j, i: (0, i, b * nt + j)),
            pl.BlockSpec((PEER_HEADS, ei, t_tile), lambda b, j, i: (0, i, b * nt + j)),
            pl.BlockSpec((blk, D), lambda b, j, i: (i, 0)),
            pl.BlockSpec((blk, D), lambda b, j, i: (i, 0)),
            pl.BlockSpec((1, t_tile, D), row3),
            _mod_spec(per_row, t_tile),
        ],
        out_specs=pl.BlockSpec((1, t_tile, D), row3),
        out_shape=f((B, S, D), F32),
        scratch_shapes=[pltpu.VMEM((t_tile, D), F32), pltpu.VMEM((blk, t_tile), BF16)],
        compiler_params=_cparams(("parallel", "parallel", "arbitrary")),
        name="peer_dense",
    )(hT, s1, bb, cc, aa, u_bf, v_bf, x, ga)


def _mm_kernel(*refs, n_in, prologue, has_bias, has_res):
    it = iter(refs)
    a_refs = [next(it) for _ in range(n_in)]
    if prologue == "normmod":
        g_ref, sc_ref, sh_ref = next(it), next(it), next(it)
    w_ref = next(it)
    b_ref = next(it) if has_bias else None
    if has_res:
        x_ref, ga_ref = next(it), next(it)
    o_ref, h_ref = next(it), next(it)

    @pl.when(pl.program_id(2) == 0)
    def _():
        a = a_refs[0][0]
        for r in a_refs[1:]:
            a = a + r[0]
        if prologue == "normmod":
            a = _normmod(a, g_ref[...], sc_ref[0], sh_ref[0])
        elif prologue == "silu":
            a = a / (1.0 + jnp.exp(-a))
        h_ref[...] = a.astype(BF16)

    y = jnp.dot(h_ref[...], w_ref[...], preferred_element_type=F32)
    if has_bias:
        y = y + b_ref[...]
    if has_res:
        y = x_ref[0] + ga_ref[0] * y
    o_ref[0] = y.astype(o_ref.dtype)


def _mm(a_list, w, *, tm, tn=None, norm=None, prologue=None, bias=None, res=None, name="mm"):
    B, S, K = a_list[0].shape
    N = w.shape[1]
    tn = N if tn is None else tn
    grid = (B, S // tm, N // tn)
    ins, specs = [], []
    for a in a_list:
        ins.append(a)
        specs.append(pl.BlockSpec((1, tm, K), lambda b, j, n: (b, j, 0)))

    def mod_spec(m, width, tiled_n):
        if m.shape[1] != 1:
            return pl.BlockSpec((1, tm, width), (lambda b, j, n: (0, j, n)) if tiled_n else (lambda b, j, n: (0, j, 0)))
        return pl.BlockSpec((1, 1, width), (lambda b, j, n: (b, 0, n)) if tiled_n else (lambda b, j, n: (b, 0, 0)))

    if norm is not None:
        g, sc, sh = norm
        prologue = "normmod"
        ins += [g, sc, sh]
        specs += [pl.BlockSpec((1, K), lambda b, j, n: (0, 0)), mod_spec(sc, K, False), mod_spec(sh, K, False)]
    ins.append(w)
    specs.append(pl.BlockSpec((K, tn), lambda b, j, n: (0, n)))
    if bias is not None:
        ins.append(bias)
        specs.append(pl.BlockSpec((1, tn), lambda b, j, n: (0, n)))
    if res is not None:
        x, ga = res
        ins += [x, ga]
        specs += [pl.BlockSpec((1, tm, tn), lambda b, j, n: (b, j, n)), mod_spec(ga, tn, True)]
    return pl.pallas_call(
        functools.partial(_mm_kernel, n_in=len(a_list), prologue=prologue,
                          has_bias=bias is not None, has_res=res is not None),
        grid=grid, in_specs=specs,
        out_specs=pl.BlockSpec((1, tm, tn), lambda b, j, n: (b, j, n)),
        out_shape=jax.ShapeDtypeStruct((B, S, N), F32),
        scratch_shapes=[pltpu.VMEM((tm, K), BF16)],
        compiler_params=_cparams(("parallel", "parallel", "arbitrary")),
        name=name,
    )(*ins)


def _rmsnorm_kernel(x_ref, g_ref, o_ref):
    x = x_ref[0]
    r = lax.rsqrt(jnp.mean(x * x, axis=-1, keepdims=True) + RMS_EPS)
    o_ref[0] = (x * r) * g_ref[...]


def _rmsnorm(x, g, *, tm):
    B, S, D = x.shape
    return pl.pallas_call(
        _rmsnorm_kernel, grid=(B, S // tm),
        in_specs=[pl.BlockSpec((1, tm, D), lambda b, j: (b, j, 0)), pl.BlockSpec((1, D), lambda b, j: (0, 0))],
        out_specs=pl.BlockSpec((1, tm, D), lambda b, j: (b, j, 0)),
        out_shape=jax.ShapeDtypeStruct((B, S, D), F32),
        compiler_params=_cparams(("parallel", "parallel")), name="final_norm",
    )(x, g)


def _chunk_proj(x_ref, w1_ref, n_chunks, p_ref):
    acc = None
    for s2 in range(CMP_STRIDE // 2):
        l0 = x_ref[pl.ds(2 * s2, n_chunks, stride=CMP_STRIDE), :]
        l1 = x_ref[pl.ds(2 * s2 + 1, n_chunks, stride=CMP_STRIDE), :]
        lhs = jnp.concatenate([l0, l1], axis=-1).astype(BF16)
        d = jnp.dot(lhs, w1_ref[0, s2], preferred_element_type=F32)
        acc = d if acc is None else acc + d
    p_ref[0:n_chunks, :] = acc


def _cmp_finish(p_ref, nb, b1_ref, w2T_ref, o_ref):
    for gp in range(2):
        c0 = gp * 2 * CMP_HID
        acc = p_ref[0:nb, c0:c0 + CMP_HID] + p_ref[1:nb + 1, c0 + CMP_HID:c0 + 2 * CMP_HID]
        hid = _gelu(acc + b1_ref[0]).astype(BF16)
        kT = lax.dot_general(w2T_ref[0], hid, (((1,), (1,)), ((), ())), preferred_element_type=F32)
        o_ref[0, 0, gp] = kT.astype(o_ref.dtype)


def _cmp_prompt_kernel(x_ref, w1_ref, b1_ref, w2T_ref, o_ref, p_ref, *, n_chunks, nbp):
    p_ref[n_chunks:, :] = jnp.zeros((p_ref.shape[0] - n_chunks, p_ref.shape[1]), F32)
    _chunk_proj(x_ref.at[0], w1_ref, n_chunks, p_ref)
    _cmp_finish(p_ref, nbp, b1_ref, w2T_ref, o_ref)


def _cmp_sample_kernel(pt_ref, *refs, n_pages, n_chunks):
    page_refs = refs[:n_pages]
    xn_ref, w1_ref, b1_ref, w2T_ref, o_ref, xs_ref, p_ref = refs[n_pages:]
    for p in range(n_pages):
        xs_ref[p * PAGE_SIZE:(p + 1) * PAGE_SIZE, :] = page_refs[p][0, 0, 0].T
    _chunk_proj(xs_ref, w1_ref, n_chunks, p_ref)
    new = jnp.dot(xn_ref[0].astype(BF16), w1_ref[0, 0, 0:LANES, :], preferred_element_type=F32)
    p_ref[n_chunks:n_chunks + SUBLANES, :] = new
    _cmp_finish(p_ref, n_chunks, b1_ref, w2T_ref, o_ref)


def _cmp_weights(w1, w2):
    w1r = w1.reshape(2, 2, CMP_STRIDE, HEAD_DIM, CMP_HID).transpose(0, 2, 3, 1, 4)
    w1r = w1r.reshape(2, CMP_STRIDE, HEAD_DIM, 2 * CMP_HID)
    eye = jnp.eye(2, dtype=F32)
    wp = jnp.einsum("ab,ksdn->ksadbn", eye, w1r).reshape(2, CMP_STRIDE // 2, 4 * HEAD_DIM, 4 * CMP_HID)
    return wp.astype(BF16), w2.transpose(0, 2, 1).astype(BF16)


def _cmp_prompt(kv_c, w1p, b1, w2T):
    B, S, _ = kv_c.shape
    n_chunks = S // CMP_STRIDE
    nbp = n_chunks
    return pl.pallas_call(
        functools.partial(_cmp_prompt_kernel, n_chunks=n_chunks, nbp=nbp),
        grid=(B, 2, 2),
        in_specs=[
            pl.BlockSpec((1, S, LANES), lambda b, k, gp: (b, 0, k * 2 + gp)),
            pl.BlockSpec((1, CMP_STRIDE // 2, 2 * LANES, 4 * CMP_HID), lambda b, k, gp: (k, 0, 0, 0)),
            pl.BlockSpec((1, 1, CMP_HID), lambda b, k, gp: (k, 0, 0)),
            pl.BlockSpec((1, HEAD_DIM, CMP_HID), lambda b, k, gp: (k, 0, 0)),
        ],
        out_specs=pl.BlockSpec((1, 1, 2, HEAD_DIM, nbp), lambda b, k, gp: (b, k, gp, 0, 0)),
        out_shape=jax.ShapeDtypeStruct((B, 2, G_A, HEAD_DIM, nbp), BF16),
        scratch_shapes=[pltpu.VMEM((n_chunks + SUBLANES, 4 * CMP_HID), F32)],
        compiler_params=_cparams(("parallel", "parallel", "parallel")), name="cmp_prompt",
    )(kv_c, w1p, b1, w2T)


def _cmp_sample(cacheT, page_table, kv_c_new, w1p, b1, w2T):
    Bd, n_pages = page_table.shape
    n_chunks = n_pages * PAGE_SIZE // CMP_STRIDE

    def page_spec(p):
        return pl.BlockSpec((1, 1, 1, LANES, PAGE_SIZE), lambda b, k, gp, pt: (pt[b, p], k, gp, 0, 0))

    return pl.pallas_call(
        functools.partial(_cmp_sample_kernel, n_pages=n_pages, n_chunks=n_chunks),
        grid_spec=pltpu.PrefetchScalarGridSpec(
            num_scalar_prefetch=1, grid=(Bd, 2, 2),
            in_specs=[page_spec(p) for p in range(n_pages)] + [
                pl.BlockSpec((1, SUBLANES, LANES), lambda b, k, gp, pt: (b, 0, k * 2 + gp)),
                pl.BlockSpec((1, CMP_STRIDE // 2, 2 * LANES, 4 * CMP_HID), lambda b, k, gp, pt: (k, 0, 0, 0)),
                pl.BlockSpec((1, 1, CMP_HID), lambda b, k, gp, pt: (k, 0, 0)),
                pl.BlockSpec((1, HEAD_DIM, CMP_HID), lambda b, k, gp, pt: (k, 0, 0)),
            ],
            out_specs=pl.BlockSpec((1, 1, 2, HEAD_DIM, n_chunks), lambda b, k, gp, pt: (b, k, gp, 0, 0)),
            scratch_shapes=[pltpu.VMEM((n_pages * PAGE_SIZE, LANES), F32),
                            pltpu.VMEM((n_chunks + SUBLANES, 4 * CMP_HID), F32)],
        ),
        out_shape=jax.ShapeDtypeStruct((Bd, 2, G_A, HEAD_DIM, n_chunks), BF16),
        compiler_params=_cparams(("parallel", "parallel", "parallel")), name="cmp_sample",
    )(page_table, *([cacheT] * n_pages), kv_c_new, w1p, b1, w2T)


_LOWEST = -3.0e38


def _select_rounds(score, lane):
    picks = []
    for _ in range(N_SEL):
        m = jnp.max(score, axis=-1, keepdims=True)
        idx = jnp.min(jnp.where(score == m, lane, 1 << 20), axis=-1, keepdims=True)
        picks.append(idx)
        score = jnp.where(lane == idx, _LOWEST, score)
    return picks


def _overlap_matrix(nb_pad, ns_pad, nb, ns):
    cs = jnp.arange(nb_pad) * CMP_STRIDE
    ss = jnp.arange(ns_pad) * SEL_BLOCK
    ov = jnp.maximum(jnp.minimum(cs[:, None] + CMP_LEN, ss[None, :] + SEL_BLOCK)
                     - jnp.maximum(cs[:, None], ss[None, :]), 0).astype(F32) / CMP_LEN
    ov = jnp.where((jnp.arange(nb_pad)[:, None] < nb) & (jnp.arange(ns_pad)[None, :] < ns), ov, 0.0)
    return ov.astype(BF16)


def _row_info(rows, tq, q0):
    row = lax.broadcasted_iota(jnp.int32, (rows, 1), 0)
    return q0 + (row & (tq - 1))


def _cmp_attend_kernel(q_ref, kT_ref, vT_ref, slope_ref, gate_ref, ov_ref, o_ref, bits_ref, *, tq, nb, ns):
    R = q_ref.shape[2]
    rows = R * tq
    q0 = pl.program_id(2) * tq
    q = q_ref[0, 0].reshape(rows, HEAD_DIM)
    qpos = _row_info(rows, tq, q0)
    slope = slope_ref[0]
    nbp = kT_ref.shape[-1]
    blk_end = lax.broadcasted_iota(jnp.int32, (1, nbp), 1) * CMP_STRIDE + (CMP_LEN - 1)
    s = jnp.dot(q, kT_ref[0, 0, 0], preferred_element_type=F32)
    s = s + slope * (blk_end - q0).astype(F32)
    valid = (blk_end <= qpos) & (blk_end < nb * CMP_STRIDE + CMP_LEN - 1)
    s = jnp.where(valid, s, NEG)
    m = jnp.max(s, axis=-1, keepdims=True)
    e = jnp.exp(s - m)
    p = e * (jnp.where(qpos >= CMP_LEN - 1, 1.0, 0.0) / jnp.sum(e, axis=-1, keepdims=True))
    pb = p.astype(BF16)
    o = lax.dot_general(pb, vT_ref[0, 0, 0], (((1,), (1,)), ((), ())), preferred_element_type=F32)
    imp_r = jnp.dot(pb, ov_ref[...], preferred_element_type=F32)
    g = gate_ref[0, 0]
    imp = None
    for r in range(R):
        gate = 1.0 / (1.0 + jnp.exp(-g[:, r:r + 1]))
        o_ref[0, 0, r] = o[r * tq:(r + 1) * tq] * gate
        part = imp_r[r * tq:(r + 1) * tq]
        imp = part if imp is None else imp + part
    tpos = qpos[0:tq]
    lane = lax.broadcasted_iota(jnp.int32, (1, imp.shape[1]), 1)
    cur = tpos >> 6
    forced = (lane == 0) | (lane == cur) | (lane == cur - 1)
    score = jnp.where(lane * SEL_BLOCK <= tpos, imp + jnp.where(forced, FORCE, 0.0), NEG)
    score = jnp.where(lane < ns, score, _LOWEST)
    bits = jnp.zeros((tq, 1), jnp.int32)
    for idx in _select_rounds(score, lane):
        bits = bits | (1 << idx)
    bits_ref[0, 0] = jnp.broadcast_to(bits, (tq, LANES))


def _cmp_attend_prompt(q_t, kvcT, slope_rows, gates_t, ov, *, tq, nb, ns):
    B, G, R, S, _ = q_t.shape
    nbp = kvcT.shape[-1]
    return pl.pallas_call(
        functools.partial(_cmp_attend_kernel, tq=tq, nb=nb, ns=ns),
        grid=(B, G, S // tq),
        in_specs=[
            pl.BlockSpec((1, 1, R, tq, HEAD_DIM), lambda b, g, i: (b, g, 0, i, 0)),
            pl.BlockSpec((1, 1, 1, HEAD_DIM, nbp), lambda b, g, i: (b, 0, g, 0, 0)),
            pl.BlockSpec((1, 1, 1, HEAD_DIM, nbp), lambda b, g, i: (b, 1, g, 0, 0)),
            pl.BlockSpec((1, R * tq, 1), lambda b, g, i: (g, 0, 0)),
            pl.BlockSpec((1, 1, tq, 16), lambda b, g, i: (b, g, i, 0)),
            pl.BlockSpec(ov.shape, lambda b, g, i: (0, 0)),
        ],
        out_specs=[
            pl.BlockSpec((1, 1, R, tq, HEAD_DIM), lambda b, g, i: (b, g, 0, i, 0)),
            pl.BlockSpec((1, 1, tq, LANES), lambda b, g, i: (b, g, i, 0)),
        ],
        out_shape=[jax.ShapeDtypeStruct((B, G, R, S, HEAD_DIM), F32),
                   jax.ShapeDtypeStruct((B, G, S, LANES), jnp.int32)],
        compiler_params=_cparams(("parallel", "parallel", "parallel")), name="cmp_attend_prompt",
    )(q_t, kvcT, kvcT, slope_rows, gates_t, ov)


def _flash_kernel(*refs, tq, tk, window, use_bits, gate_col, use_sink):
    it = iter(refs)
    q_ref, kT_ref, v_ref, slope_ref = next(it), next(it), next(it), next(it)
    bits_ref = next(it) if use_bits else None
    gate_ref = next(it) if gate_col is not None else None
    sink_ref = next(it) if use_sink else None
    o_ref = next(it)
    R = q_ref.shape[2]
    rows = R * tq
    qi = pl.program_id(2)
    q0 = qi * tq
    q = q_ref[0, 0].reshape(rows, HEAD_DIM)
    qpos = _row_info(rows, tq, q0)
    slope = slope_ref[0]
    if use_bits:
        b1 = bits_ref[0, 0][:, 0:1]
        bits = jnp.concatenate([b1] * R, axis=0)
    hi = (q0 + tq + tk - 1) // tk
    lo = 0 if window is None else jnp.maximum(q0 - (window - 1), 0) // tk

    def body(j, carry):
        m, l, acc = carry
        k0 = pl.multiple_of(j * tk, tk)
        kT = kT_ref[0, 0, :, pl.ds(k0, tk)]
        v = v_ref[0, 0, pl.ds(k0, tk), :]
        kpos = k0 + lax.broadcasted_iota(jnp.int32, (1, tk), 1)
        s = jnp.dot(q, kT, preferred_element_type=F32)
        s = s + slope * (kpos - q0).astype(F32)
        mask = kpos <= qpos
        if window is not None:
            mask = mask & (kpos > qpos - window)
        if use_bits:
            mask = mask & ((bits & (1 << (kpos >> 6))) != 0)
        s = jnp.where(mask, s, NEG)
        m_new = jnp.maximum(m, jnp.max(s, axis=-1, keepdims=True))
        alpha = jnp.exp(m - m_new)
        p = jnp.exp(s - m_new)
        l = alpha * l + jnp.sum(p, axis=-1, keepdims=True)
        acc = alpha * acc + jnp.dot(p.astype(BF16), v, preferred_element_type=F32)
        return m_new, l, acc

    m, l, acc = lax.fori_loop(lo, hi, body, (jnp.full((rows, 1), NEG, F32), jnp.zeros((rows, 1), F32),
                                             jnp.zeros((rows, HEAD_DIM), F32)))
    if use_sink:
        sk = sink_ref[0] + slope * (qpos - q0).astype(F32)
        m_new = jnp.maximum(m, sk)
        alpha = jnp.exp(m - m_new)
        l = alpha * l + jnp.exp(sk - m_new)
        acc = alpha * acc
    o = acc / l
    for r in range(R):
        orr = o[r * tq:(r + 1) * tq]
        if gate_col is not None:
            c = gate_col * R + r
            orr = orr * (1.0 / (1.0 + jnp.exp(-gate_ref[0, 0][:, c:c + 1])))
        o_ref[0, 0, r] = orr


def _flash(q_t, kT, v, slope_rows, *, tq, tk, window=None, bits=None, gates_t=None, gate_col=None,
           sink_rows=None, name="flash"):
    B, G, R, S, _ = q_t.shape
    ins = [q_t, kT, v, slope_rows]
    specs = [
        pl.BlockSpec((1, 1, R, tq, HEAD_DIM), lambda b, g, i: (b, g, 0, i, 0)),
        pl.BlockSpec((1, 1, HEAD_DIM, S), lambda b, g, i: (b, g, 0, 0)),
        pl.BlockSpec((1, 1, S, HEAD_DIM), lambda b, g, i: (b, g, 0, 0)),
        pl.BlockSpec((1, R * tq, 1), lambda b, g, i: (g, 0, 0)),
    ]
    if bits is not None:
        ins.append(bits)
        specs.append(pl.BlockSpec((1, 1, tq, LANES), lambda b, g, i: (b, g, i, 0)))
    if gates_t is not None:
        ins.append(gates_t)
        specs.append(pl.BlockSpec((1, 1, tq, 16), lambda b, g, i: (b, g, i, 0)))
    if sink_rows is not None:
        ins.append(sink_rows)
        specs.append(pl.BlockSpec((1, R * tq, 1), lambda b, g, i: (g, 0, 0)))
    return pl.pallas_call(
        functools.partial(_flash_kernel, tq=tq, tk=tk, window=window, use_bits=bits is not None,
                          gate_col=gate_col if gates_t is not None else None, use_sink=sink_rows is not None),
        grid=(B, G, S // tq), in_specs=specs,
        out_specs=pl.BlockSpec((1, 1, R, tq, HEAD_DIM), lambda b, g, i: (b, g, 0, i, 0)),
        out_shape=jax.ShapeDtypeStruct((B, G, R, S, HEAD_DIM), F32),
        compiler_params=_cparams(("parallel", "parallel", "parallel")), name=name,
    )(*ins)


def _softmax_parts(s, valid, extra_s, extra_valid, sink):
    s = jnp.where(valid, s, NEG)
    m = jnp.max(s, axis=-1, keepdims=True)
    if extra_s is not None:
        extra_s = jnp.where(extra_valid, extra_s, NEG)
        m = jnp.maximum(m, extra_s)
    if sink is not None:
        m = jnp.maximum(m, sink)
    e = jnp.exp(s - m)
    l = jnp.sum(e, axis=-1, keepdims=True)
    e_new = None
    if extra_s is not None:
        e_new = jnp.where(extra_valid, jnp.exp(extra_s - m), 0.0)
        l = l + e_new
    if sink is not None:
        l = l + jnp.exp(sink - m)
    return e, e_new, l


def _new_key_score(q16, k_new):
    return jnp.sum(q16.astype(F32) * k_new.astype(BF16).astype(F32), axis=-1, keepdims=True)


def _dec_cmp_kernel(q_ref, kT_ref, vT_ref, slope_ref, gate_ref, ov_ref, o_ref, idx_ref, *, qpos, n_past_blocks):
    q16 = q_ref[0]
    slope = slope_ref[...]
    nb = kT_ref.shape[-1]
    nsp = ov_ref.shape[1]
    blk_end = lax.broadcasted_iota(jnp.int32, (1, nb), 1) * CMP_STRIDE + (CMP_LEN - 1)
    valid = blk_end <= qpos
    bias = slope * (blk_end - qpos).astype(F32)
    row = lax.broadcasted_iota(jnp.int32, (H_A, 1), 0)
    lane = lax.broadcasted_iota(jnp.int32, (1, nsp), 1)
    cur = qpos // SEL_BLOCK
    forced = (lane == 0) | (lane == cur) | (lane == cur - 1)
    ns = n_past_blocks + 1
    o = jnp.zeros((H_A, HEAD_DIM), F32)
    tile_r = lax.broadcasted_iota(jnp.int32, (SUBLANES, LANES), 0)
    tile_l = lax.broadcasted_iota(jnp.int32, (SUBLANES, LANES), 1)
    tile = jnp.zeros((SUBLANES, LANES), jnp.int32)
    for g in range(G_A):
        s = jnp.dot(q16, kT_ref[0, 0, g], preferred_element_type=F32) + bias
        e, _, l = _softmax_parts(s, valid, None, None, None)
        pb = (e / l).astype(BF16)
        og = lax.dot_general(pb, vT_ref[0, 0, g], (((1,), (1,)), ((), ())), preferred_element_type=F32)
        mine = (row >> 2) == g
        o = jnp.where(mine, og, o)
        imp16 = jnp.dot(pb, ov_ref[...], preferred_element_type=F32)
        imp = jnp.sum(jnp.where(mine, imp16, 0.0), axis=0, keepdims=True)
        score = jnp.where(lane * SEL_BLOCK <= qpos, imp + jnp.where(forced, FORCE, 0.0), NEG)
        score = jnp.where(lane < ns, score, _LOWEST)
        for r, idx in enumerate(_select_rounds(score, lane)):
            tile = jnp.where((tile_r == g) & (tile_l == r), idx, tile)
    gate = 1.0 / (1.0 + jnp.exp(-gate_ref[0][:, 0:1]))
    o_ref[0] = o * gate
    idx_ref[0] = tile


def _dec_cmp(q16, kvcT, slope16, gates16, ov, *, qpos, n_past_blocks):
    Bd = q16.shape[0]
    nb = kvcT.shape[-1]
    return pl.pallas_call(
        functools.partial(_dec_cmp_kernel, qpos=qpos, n_past_blocks=n_past_blocks),
        grid=(Bd,),
        in_specs=[
            pl.BlockSpec((1, H_A, HEAD_DIM), lambda b: (b, 0, 0)),
            pl.BlockSpec((1, 1, G_A, HEAD_DIM, nb), lambda b: (b, 0, 0, 0, 0)),
            pl.BlockSpec((1, 1, G_A, HEAD_DIM, nb), lambda b: (b, 1, 0, 0, 0)),
            pl.BlockSpec((H_A, 1), lambda b: (0, 0)),
            pl.BlockSpec((1, H_A, 4), lambda b: (b, 0, 0)),
            pl.BlockSpec(ov.shape, lambda b: (0, 0)),
        ],
        out_specs=[pl.BlockSpec((1, H_A, HEAD_DIM), lambda b: (b, 0, 0)),
                   pl.BlockSpec((1, SUBLANES, LANES), lambda b: (b, 0, 0))],
        out_shape=[jax.ShapeDtypeStruct((Bd, H_A, HEAD_DIM), F32),
                   jax.ShapeDtypeStruct((Bd, SUBLANES, LANES), jnp.int32)],
        compiler_params=_cparams(("parallel",)), name="dec_cmp",
    )(q16, kvcT, kvcT, slope16, gates16, ov)


def _dec_sel_kernel(idx_ref, pt_ref, *refs, qpos, n_past_blocks):
    k_refs = refs[:N_SEL]
    v_refs = refs[N_SEL:2 * N_SEL]
    q_ref, kn_ref, vn_ref, slope_ref, gate_ref, o_ref = refs[2 * N_SEL:]
    b = pl.program_id(0)
    g = pl.program_id(1)
    q16 = q_ref[0]
    slope = slope_ref[...]
    lane = lax.broadcasted_iota(jnp.int32, (1, PAGE_SIZE), 1)
    s_l, valid_l, v_l = [], [], []
    any_new = None
    for kk in range(N_SEL):
        idx = idx_ref[(b * G_A + g) * N_SEL + kk]
        is_new = (jnp.zeros((H_A, 1), jnp.int32) + idx) >= n_past_blocks
        any_new = is_new if any_new is None else (any_new | is_new)
        jp = jnp.minimum(idx, n_past_blocks - 1)
        kpos = (jp >> 1) * PAGE_SIZE + lane
        s = jnp.dot(q16, k_refs[kk][0, 0, 0].astype(BF16), preferred_element_type=F32)
        s_l.append(s - slope * (qpos - kpos).astype(F32))
        in_block = ((lane >> 6) == (jp & 1)) & ((lane * 0 + idx) < n_past_blocks) & (kpos <= qpos)
        valid_l.append(jnp.broadcast_to(in_block, (H_A, PAGE_SIZE)))
        v_l.append(v_refs[kk][0, 0, 0].astype(BF16))
    s = jnp.concatenate(s_l, axis=-1)
    valid = jnp.concatenate(valid_l, axis=-1)
    vT = jnp.concatenate(v_l, axis=-1)
    s_new = _new_key_score(q16, kn_ref[0, 0, 0])
    e, e_new, l = _softmax_parts(s, valid, s_new, any_new, None)
    o = lax.dot_general(e.astype(BF16), vT, (((1,), (1,)), ((), ())), preferred_element_type=F32)
    o = o + e_new * vn_ref[0, 0, 0].astype(BF16).astype(F32)
    gate = 1.0 / (1.0 + jnp.exp(-gate_ref[0][:, 1:2]))
    o_ref[0, 0] = (o / l) * gate


def _dec_sel(idx_flat, page_table, cacheT, q16, kv_new, slope16, gates16, *, qpos, n_past_blocks):
    Bd = q16.shape[0]

    def tile_spec(kk, kind):
        def imap(b, g, idx, pt):
            jp = jnp.minimum(idx[(b * G_A + g) * N_SEL + kk], n_past_blocks - 1)
            return (pt[b, jp >> 1], kind, g, 0, 0)
        return pl.BlockSpec((1, 1, 1, HEAD_DIM, PAGE_SIZE), imap)

    return pl.pallas_call(
        functools.partial(_dec_sel_kernel, qpos=qpos, n_past_blocks=n_past_blocks),
        grid_spec=pltpu.PrefetchScalarGridSpec(
            num_scalar_prefetch=2, grid=(Bd, G_A),
            in_specs=[tile_spec(kk, 0) for kk in range(N_SEL)] + [tile_spec(kk, 1) for kk in range(N_SEL)] + [
                pl.BlockSpec((1, H_A, HEAD_DIM), lambda b, g, idx, pt: (b, 0, 0)),
                pl.BlockSpec((1, 1, 1, 1, HEAD_DIM), lambda b, g, idx, pt: (b, 0, g, 0, 0)),
                pl.BlockSpec((1, 1, 1, 1, HEAD_DIM), lambda b, g, idx, pt: (b, 1, g, 0, 0)),
                pl.BlockSpec((H_A, 1), lambda b, g, idx, pt: (0, 0)),
                pl.BlockSpec((1, H_A, 4), lambda b, g, idx, pt: (b, 0, 0)),
            ],
            out_specs=pl.BlockSpec((1, 1, H_A, HEAD_DIM), lambda b, g, idx, pt: (b, g, 0, 0)),
        ),
        out_shape=jax.ShapeDtypeStruct((Bd, G_A, H_A, HEAD_DIM), F32),
        compiler_params=_cparams(("parallel", "parallel")), name="dec_sel",
    )(idx_flat, page_table, *([cacheT] * (2 * N_SEL)), q16, kv_new, kv_new, slope16, gates16)


def _dec_slab_kernel(*refs, n_groups, window, qpos, gate_col, use_sink):
    it = iter(refs)
    q_ref, kT_ref, vT_ref, kn_ref, vn_ref, slope_ref = (next(it) for _ in range(6))
    gate_ref = next(it) if gate_col is not None else None
    sink_ref = next(it) if use_sink else None
    o_ref = next(it)
    q16 = q_ref[0]
    slope = slope_ref[...]
    nbuf = kT_ref.shape[-1]
    rpg = q16.shape[0] // n_groups
    kpos = qpos - nbuf + lax.broadcasted_iota(jnp.int32, (1, nbuf), 1)
    dist = qpos - kpos
    valid = (dist >= 0) & (dist < window) & (kpos >= 0)
    bias = -slope * dist.astype(F32)
    row = lax.broadcasted_iota(jnp.int32, (q16.shape[0], 1), 0)
    sink = sink_ref[...] if use_sink else None
    o = jnp.zeros((q16.shape[0], HEAD_DIM), F32)
    for g in range(n_groups):
        s = jnp.dot(q16, kT_ref[0, 0, g].astype(BF16), preferred_element_type=F32) + bias
        s_new = _new_key_score(q16, kn_ref[0, 0, g:g + 1, :])
        e, e_new, l = _softmax_parts(s, valid, s_new, True, sink)
        og = lax.dot_general(e.astype(BF16), vT_ref[0, 0, g].astype(BF16), (((1,), (1,)), ((), ())),
                             preferred_element_type=F32)
        og = (og + e_new * vn_ref[0, 0, g:g + 1, :].astype(BF16).astype(F32)) / l
        o = jnp.where((row // rpg) == g, og, o)
    if gate_col is not None:
        o = o * (1.0 / (1.0 + jnp.exp(-gate_ref[0][:, gate_col:gate_col + 1])))
    o_ref[0] = o


def _dec_slab(q16, slabT, kv_new, slope16, *, window, qpos, gates16=None, gate_col=None, sink16=None, name):
    Bd, _, G, _, nbuf = slabT.shape
    H = q16.shape[1]
    ins = [q16, slabT, slabT, kv_new, kv_new, slope16]
    specs = [
        pl.BlockSpec((1, H, HEAD_DIM), lambda b: (b, 0, 0)),
        pl.BlockSpec((1, 1, G, HEAD_DIM, nbuf), lambda b: (b, 0, 0, 0, 0)),
        pl.BlockSpec((1, 1, G, HEAD_DIM, nbuf), lambda b: (b, 1, 0, 0, 0)),
        pl.BlockSpec((1, 1, G, HEAD_DIM), lambda b: (b, 0, 0, 0)),
        pl.BlockSpec((1, 1, G, HEAD_DIM), lambda b: (b, 1, 0, 0)),
        pl.BlockSpec((H, 1), lambda b: (0, 0)),
    ]
    if gates16 is not None:
        ins.append(gates16)
        specs.append(pl.BlockSpec((1, H, 4), lambda b: (b, 0, 0)))
    if sink16 is not None:
        ins.append(sink16)
        specs.append(pl.BlockSpec((H, 1), lambda b: (0, 0)))
    return pl.pallas_call(
        functools.partial(_dec_slab_kernel, n_groups=G, window=window, qpos=qpos,
                          gate_col=gate_col if gates16 is not None else None, use_sink=sink16 is not None),
        grid=(Bd,), in_specs=specs,
        out_specs=pl.BlockSpec((1, H, HEAD_DIM), lambda b: (b, 0, 0)),
        out_shape=jax.ShapeDtypeStruct((Bd, H, HEAD_DIM), F32),
        compiler_params=_cparams(("parallel",)), name=name,
    )(*ins)


def _alibi(n_heads):
    return 2.0 ** (-8.0 * jnp.arange(1, n_heads + 1, dtype=F32) / n_heads)


def _heads_first(z, G, R):
    B, S, _ = z.shape
    return z.reshape(B, S, G, R, HEAD_DIM).transpose(0, 2, 3, 1, 4)


def _heads_last(o):
    B, G, R, S, _ = o.shape
    return o.transpose(0, 3, 1, 2, 4).reshape(B, S, G * R * HEAD_DIM)


def kernel(x_prompt, x_sample, c_prompt, c_sample, cache_cmp_kv, cache_sel_kv, state_win_kv, state_shared_kv, page_table, w_ada, b_ada, g_norm, w_in_a, w_o_a, cmp_pos, w_cmp1, b_cmp1, w_cmp2, g_kv, w_ada_kv, b_ada_kv, w_kv_b, w_q_b, w_o_b, sinks, w_pq, peer_keys, peer_u, peer_v, g_final):
    B, S, D = x_prompt.shape
    Bd = x_sample.shape[0]
    n_pages = page_table.shape[1]
    past_len = n_pages * PAGE_SIZE
    n_past_blocks = past_len // SEL_BLOCK
    TQ = 128
    bf = lambda a: a.astype(BF16)

    c_all = jnp.concatenate([c_prompt, c_sample], axis=0)[None]
    nc = B + Bd

    def ada(w, b):
        return _mm([c_all], bf(w), tm=nc, tn=2048, prologue="silu", bias=b[None], name="ada")[0]

    mods = [ada(w_ada[i], b_ada[i]) for i in range(2)]
    mods_kv = ada(w_ada_kv, b_ada_kv)

    def split_mods(m, n):
        parts = jnp.split(m, n, axis=-1)
        return [p[:B, None, :] for p in parts], [p[None, B:, :] for p in parts]

    mp0, ms0 = split_mods(mods[0], 6)
    mp1, ms1 = split_mods(mods[1], 6)
    mpk, msk = split_mods(mods_kv, 2)

    w_in = bf(jnp.pad(w_in_a[0], ((0, 0), (0, 21 * LANES - NSA_IN))))
    w1p, w2T = _cmp_weights(w_cmp1[0], w_cmp2[0])
    pos_rows = jnp.pad(cmp_pos[0].reshape(2, 1, CMP_LEN * HEAD_DIM), ((0, 0), (0, SUBLANES - 1), (0, 0)))
    b1 = jnp.stack([_mm([pos_rows[k:k + 1]], bf(w_cmp1[0, k]), tm=SUBLANES, bias=b_cmp1[0, k][None],
                        name="cmp_bias")[0, 0:1] for k in range(2)])
    slopes_a = _alibi(H_A)
    slopes_b = _alibi(H_B)
    gn = lambda v: v.reshape(1, D)
    peer_w = [(bf(w_pq[i].T), bf(peer_keys[i].reshape(2 * PEER_HEADS, N_KEYS, N_KEYS)), bf(peer_u[i]), bf(peer_v[i]))
              for i in range(2)]

    def rows_of(v, G, R):
        return jnp.repeat(v.reshape(G, R), TQ, axis=1)[..., None]

    z = _mm([x_prompt], w_in, tm=512, norm=(gn(g_norm[0, 0]), mp0[1], mp0[0]), name="nsa_in_p")
    kv_c, kv_s, kv_w = (z[..., NSA_Q + i * NSA_KV:NSA_Q + (i + 1) * NSA_KV] for i in range(3))
    q_t = bf(_heads_first(z[..., :NSA_Q] * (HEAD_DIM ** -0.5), G_A, R_A))
    glog = z[..., NSA_Q + 3 * NSA_KV:NSA_IN].reshape(B, S, 3, G_A, R_A).transpose(0, 3, 1, 2, 4)
    gates_t = jnp.pad(glog.reshape(B, G_A, S, 3 * R_A), ((0, 0), (0, 0), (0, 0), (0, 16 - 3 * R_A)))
    slope_rows_a = rows_of(slopes_a, G_A, R_A)

    kvcT = _cmp_prompt(kv_c, w1p, b1, w2T)
    nb_p = S // CMP_STRIDE - 1
    ns_p = S // SEL_BLOCK
    ov_p = _overlap_matrix(kvcT.shape[-1], LANES, nb_p, ns_p)
    o_c, bits = _cmp_attend_prompt(q_t, kvcT, slope_rows_a, gates_t, ov_p, tq=TQ, nb=nb_p, ns=ns_p)

    def kT_v(kv, G):
        kv5 = kv.reshape(B, S, 2, G, HEAD_DIM)
        return bf(kv5[:, :, 0].transpose(0, 2, 3, 1)), bf(kv5[:, :, 1].transpose(0, 2, 1, 3))

    kT_s, v_s = kT_v(kv_s, G_A)
    o_s = _flash(q_t, kT_s, v_s, slope_rows_a, tq=TQ, tk=256, bits=bits, gates_t=gates_t, gate_col=1, name="sel_p")
    kT_w, v_w = kT_v(kv_w, G_A)
    o_w = _flash(q_t, kT_w, v_w, slope_rows_a, tq=TQ, tk=256, window=WIN_A, gates_t=gates_t, gate_col=2, name="win_p")
    x1 = _mm([_heads_last(o_c), _heads_last(o_s), _heads_last(o_w)], bf(w_o_a[0]), tm=512,
             res=(x_prompt, mp0[2]), name="nsa_out_p")
    x1 = _peer(x1, mp0[4], mp0[3], mp0[5], gn(g_norm[0, 1]), *peer_w[0], t_tile=512, ei=8)

    kv_sh = _mm([x1], bf(w_kv_b), tm=512, norm=(gn(g_kv), mpk[1], mpk[0]), name="kv_sh_p")
    q1 = _mm([x1], bf(w_q_b[0]), tm=512, norm=(gn(g_norm[1, 0]), mp1[1], mp1[0]), name="q_b_p")
    q1_t = bf(_heads_first(q1 * (HEAD_DIM ** -0.5), G_B, R_B))
    kT_b, v_b = kT_v(kv_sh, G_B)
    o_b = _flash(q1_t, kT_b, v_b, rows_of(slopes_b, G_B, R_B), tq=TQ, tk=128, window=WIN_B,
                 sink_rows=rows_of(sinks[0], G_B, R_B), name="swa_p")
    x2 = _mm([_heads_last(o_b)], bf(w_o_b[0]), tm=512, res=(x1, mp1[2]), name="swa_out_p")
    x2 = _peer(x2, mp1[4], mp1[3], mp1[5], gn(g_norm[1, 1]), *peer_w[1], t_tile=512, ei=8)
    y_prompt = _rmsnorm(x2, gn(g_final), tm=512)

    xs = x_sample.reshape(1, Bd, D)
    zs = _mm([xs], w_in, tm=Bd, norm=(gn(g_norm[0, 0]), ms0[1], ms0[0]), name="nsa_in_s")[0]
    kvs_new = [zs[:, NSA_Q + i * NSA_KV:NSA_Q + (i + 1) * NSA_KV] for i in range(3)]
    q16 = bf(zs[:, :NSA_Q] * (HEAD_DIM ** -0.5)).reshape(Bd, H_A, HEAD_DIM)
    gates16 = jnp.pad(zs[:, NSA_Q + 3 * NSA_KV:NSA_IN].reshape(Bd, 3, H_A).transpose(0, 2, 1), ((0, 0), (0, 0), (0, 1)))
    slope16_a = slopes_a.reshape(H_A, 1)

    cmpT = cache_cmp_kv[0].transpose(0, 2, 3, 4, 1)
    n_phys = cmpT.shape[0]
    kvc_new = jnp.pad(kvs_new[0][:, None, :], ((0, 0), (0, SUBLANES - 1), (0, 0)))
    kvcT_s = _cmp_sample(cmpT.reshape(n_phys, 2, 2, 2 * HEAD_DIM, PAGE_SIZE), page_table, kvc_new, w1p, b1, w2T)
    nb_s = kvcT_s.shape[-1] - 1
    ns_pad = 2 * LANES
    ov_s = _overlap_matrix(kvcT_s.shape[-1], ns_pad, nb_s + 1, n_past_blocks + 1)
    o_cs, idx_tile = _dec_cmp(q16, kvcT_s, slope16_a, gates16, ov_s, qpos=past_len, n_past_blocks=n_past_blocks)
    idx_flat = idx_tile[:, :G_A, :N_SEL].reshape(-1)

    selT = cache_sel_kv[0].transpose(0, 2, 3, 4, 1)
    kvs_new5 = kvs_new[1].reshape(Bd, 2, G_A, 1, HEAD_DIM)
    o_ss4 = _dec_sel(idx_flat, page_table, selT, q16, kvs_new5, slope16_a, gates16,
                     qpos=past_len, n_past_blocks=n_past_blocks)
    own = (jnp.arange(H_A) // R_A)[None, :] == jnp.arange(G_A)[:, None]
    o_ss = jnp.sum(jnp.where(own[None, :, :, None], o_ss4, 0.0), axis=1)

    winT = state_win_kv[0].transpose(0, 2, 3, 4, 1)
    o_ws = _dec_slab(q16, winT, kvs_new[2].reshape(Bd, 2, G_A, HEAD_DIM), slope16_a, window=WIN_A, qpos=past_len,
                     gates16=gates16, gate_col=2, name="dec_win")
    flat = lambda o: o.reshape(1, Bd, H_A * HEAD_DIM)
    xs1 = _mm([flat(o_cs), flat(o_ss), flat(o_ws)], bf(w_o_a[0]), tm=Bd, res=(xs, ms0[2]), name="nsa_out_s")
    xs1 = _peer(xs1, ms0[4], ms0[3], ms0[5], gn(g_norm[0, 1]), *peer_w[0], t_tile=Bd, ei=8)

    kv_sh_s = _mm([xs1], bf(w_kv_b), tm=Bd, norm=(gn(g_kv), msk[1], msk[0]), name="kv_sh_s")[0]
    q1s = _mm([xs1], bf(w_q_b[0]), tm=Bd, norm=(gn(g_norm[1, 0]), ms1[1], ms1[0]), name="q_b_s")[0]
    q16b = bf(q1s * (HEAD_DIM ** -0.5)).reshape(Bd, H_B, HEAD_DIM)
    shT = state_shared_kv.transpose(0, 2, 3, 4, 1)
    o_bs = _dec_slab(q16b, shT, kv_sh_s.reshape(Bd, 2, G_B, HEAD_DIM), slopes_b.reshape(H_B, 1), window=WIN_B,
                     qpos=past_len, sink16=sinks[0].reshape(H_B, 1), name="dec_swa")
    xs2 = _mm([flat(o_bs)], bf(w_o_b[0]), tm=Bd, res=(xs1, ms1[2]), name="swa_out_s")
    xs2 = _peer(xs2, ms1[4], ms1[3], ms1[5], gn(g_norm[1, 1]), *peer_w[1], t_tile=Bd, ei=8)
    y_sample = _rmsnorm(xs2, gn(g_final), tm=Bd).reshape(Bd, 1, D)

    as_kv = lambda a, n, G: a.reshape(n, -1, 2, G, HEAD_DIM)
    nw = min(WIN_A, S)
    nwb = min(WIN_B, S)
    new_cmp_prompt = as_kv(kv_c, B, G_A)[None]
    new_sel_prompt = as_kv(kv_s, B, G_A)[None]
    new_win_prompt = as_kv(kv_w, B, G_A)[None, :, S - nw:]
    new_cmp_sample = as_kv(kvs_new[0], Bd, G_A)[None]
    new_sel_sample = as_kv(kvs_new[1], Bd, G_A)[None]
    new_win_sample = jnp.concatenate([state_win_kv[0], as_kv(kvs_new[2], Bd, G_A)], axis=1)[None, :, 1:]
    new_shared_prompt = as_kv(kv_sh, B, G_B)[:, S - nwb:]
    new_shared_sample = jnp.concatenate([state_shared_kv, as_kv(kv_sh_s, Bd, G_B)], axis=1)[:, 1:]
    return (y_prompt, y_sample, new_cmp_prompt, new_cmp_sample, new_sel_prompt, new_sel_sample,
            new_win_prompt, new_win_sample, new_shared_prompt, new_shared_sample)
```

```python
import functools

import jax
import jax.numpy as jnp
from jax import lax
from jax.experimental import pallas as pl
from jax.experimental.pallas import tpu as pltpu

F32 = jnp.float32
BF16 = jnp.bfloat16

D_MODEL = 1024
HEAD_DIM = 64
PAGE_SIZE = 128
H_A = 16
G_A = 4
R_A = H_A // G_A
CMP_LEN = 32
CMP_STRIDE = 16
CMP_HID = 128
SEL_BLOCK = 64
N_SEL = 8
WIN_A = 512
NSA_Q = H_A * HEAD_DIM
NSA_KV = 2 * G_A * HEAD_DIM
NSA_IN = NSA_Q + 3 * NSA_KV + 3 * H_A
H_B = 16
G_B = 2
R_B = H_B // G_B
WIN_B = 128
PEER_HEADS = 8
PEER_QDIM = 256
N_KEYS = 128
PEER_TOPK = 16
RMS_EPS = 1e-6
NEG = -1e30
FORCE = 1e3

LANES = 128
SUBLANES = 8
VMEM_LIMIT = 56 * 1024 * 1024


def _cparams(sem):
    return pltpu.CompilerParams(dimension_semantics=sem, vmem_limit_bytes=VMEM_LIMIT)


_GELU_C0 = 0.7978845608028654
_GELU_C1 = 0.7978845608028654 * 0.044715


def _gelu(x):
    return x * (0.5 * (1.0 + jnp.tanh(0.7978845608028654 * (x + 0.044715 * (x * x * x)))))


def _normmod(x, g, sc, sh):
    r = lax.rsqrt(jnp.mean(x * x, axis=-1, keepdims=True) + RMS_EPS)
    return (x * r) * g * (1.0 + sc) + sh


N_TOPV = PEER_TOPK + 1
_PEER_JB = 32
_PAIRS = [(p, q) for p in range(N_TOPV) for q in range(N_TOPV) if (p + 1) * (q + 1) <= N_TOPV]


def _peer_prep_kernel(x_ref, sc_ref, sh_ref, g_ref, wpqT_ref, keys_ref,
                      hT_ref, s1_ref, b_ref, c_ref, a_ref, qT_ref, topv_ref):
    T = x_ref.shape[1]
    h = _normmod(x_ref[0], g_ref[...], sc_ref[0], sh_ref[0])
    hT = h.T.astype(BF16)
    hT_ref[...] = hT
    qT_ref[...] = jnp.dot(wpqT_ref[...], hT, preferred_element_type=F32).astype(BF16)

    def per_head(hh, carry):
        for c in range(2):
            hc = hh * 2 + c
            q = qT_ref[pl.ds(pl.multiple_of(hc * N_KEYS, N_KEYS), N_KEYS), :]
            s = jnp.dot(keys_ref[hc], q, preferred_element_type=F32)
            if c == 0:
                c_ref[hh] = s
            else:
                s1_ref[hh] = s
            for r in range(N_TOPV):
                m = jnp.max(s, axis=0, keepdims=True)
                topv_ref[c, r, pl.ds(hh, 1), :] = m
                s = jnp.where(s >= m, -jnp.inf, s)
        return carry

    lax.fori_loop(0, PEER_HEADS, per_head, 0)

    for tc in range(T // LANES):
        sl = slice(tc * LANES, (tc + 1) * LANES)
        av = [topv_ref[0, p, :, sl] for p in range(N_TOPV)]
        bv = [topv_ref[1, q, :, sl] for q in range(N_TOPV)]
        cands = [av[p] + bv[q] for (p, q) in _PAIRS]
        top = av[0] + bv[0]
        work = list(cands)
        kth = None
        prev = None
        for r in range(N_TOPV):
            m = work[0]
            for w in work[1:]:
                m = jnp.maximum(m, w)
            prev, kth = kth, m
            work = [jnp.where(w >= m, -jnp.inf, w) for w in work]
        tau = 0.5 * (prev + kth)
        z = jnp.zeros_like(top)
        for cnd in cands:
            z = z + jnp.where(cnd >= tau, jnp.exp(cnd - top), 0.0)
        topv_ref[0, 1, :, sl] = tau
        topv_ref[0, 2, :, sl] = 1.0 / z

    for hh in range(PEER_HEADS):
        a1 = topv_ref[0, 0, hh:hh + 1, :]
        b1 = topv_ref[1, 0, hh:hh + 1, :]
        tau = topv_ref[0, 1, hh:hh + 1, :]
        rz = topv_ref[0, 2, hh:hh + 1, :]
        s0 = c_ref[hh]
        c_ref[hh] = tau - s0
        a_ref[hh] = jnp.exp(s0 - a1) * (0.5 * rz)
        b_ref[hh] = jnp.exp(s1_ref[hh] - b1)


def _peer_dense_kernel(hT_ref, s1_ref, b_ref, c_ref, a_ref, u_ref, v_ref, x_ref, ga_ref,
                       o_ref, acc_ref, sa_ref, sb_ref, wa_ref, wb_ref, *, n_i):
    i = pl.program_id(2)
    T = hT_ref.shape[1]
    ei = u_ref.shape[0] // N_KEYS
    s_bufs = (sa_ref, sb_ref)
    w_bufs = (wa_ref, wb_ref)

    tw = min(T, 2 * LANES)
    n_tw = T // tw

    mw = 2 * LANES
    blk = ei * N_KEYS

    def stage_a(dst):
        def piece(k, m):
            s_bufs[dst][m * mw:(m + 1) * mw, k * tw:(k + 1) * tw] = jnp.dot(
                u_ref[m * mw:(m + 1) * mw, :], hT_ref[:, k * tw:(k + 1) * tw], preferred_element_type=F32)
        return [functools.partial(piece, k, m) for k in range(n_tw) for m in range(blk // mw)]

    def stage_b(src):
        s_ref, w_ref = s_bufs[src], w_bufs[src]

        def block(ii, jb):
            jr = slice(jb * _PEER_JB, (jb + 1) * _PEER_JB)
            rows = slice(ii * N_KEYS + jb * _PEER_JB, ii * N_KEYS + (jb + 1) * _PEER_JB)
            w = jnp.zeros((_PEER_JB, T), F32)
            for hh in range(PEER_HEADS):
                cth = c_ref[hh, ii:ii + 1, :]
                ath = a_ref[hh, ii:ii + 1, :]
                w = w + jnp.where(s1_ref[hh, jr, :] >= cth, b_ref[hh, jr, :], 0.0) * ath
            s = s_ref[rows, :]
            g2 = s * (1.0 + jnp.tanh(s * (_GELU_C0 + _GELU_C1 * (s * s))))
            w_ref[rows, :] = (w * g2).astype(BF16)
        return [functools.partial(block, ii, jb) for ii in range(ei) for jb in range(N_KEYS // _PEER_JB)]

    def stage_c(src):
        def piece(k, m):
            acc_ref[k * tw:(k + 1) * tw, m * mw:(m + 1) * mw] += lax.dot_general(
                w_bufs[src][:, k * tw:(k + 1) * tw], v_ref[:, m * mw:(m + 1) * mw], (((0,), (0,)), ((), ())),
                preferred_element_type=F32)
        return [functools.partial(piece, k, m) for k in range(n_tw) for m in range(acc_ref.shape[1] // mw)]

    def run_interleaved(mxu_pieces, vpu_blocks):
        n = len(mxu_pieces)
        per = -(-len(vpu_blocks) // n)
        for k, piece in enumerate(mxu_pieces):
            piece()
            for blk_fn in vpu_blocks[k * per:(k + 1) * per]:
                blk_fn()

    @pl.when(i == 0)
    def _():
        acc_ref[...] = jnp.zeros_like(acc_ref)
        wa_ref[...] = jnp.zeros_like(wa_ref)
        wb_ref[...] = jnp.zeros_like(wb_ref)
        run_interleaved(stage_a(0), [])

    steady = (i >= 1) & (i <= n_i)
    for par in range(2):
        @pl.when(steady & (i % 2 == par))
        def _(par=par):
            run_interleaved(stage_a(par) + stage_c(par), stage_b(1 - par))

    @pl.when(i == n_i + 1)
    def _():
        run_interleaved(stage_c((n_i + 1) % 2), [])
        o_ref[0] = x_ref[0] + ga_ref[0] * acc_ref[...]


def _mod_spec(per_row, t):
    if per_row:
        return pl.BlockSpec((1, t, D_MODEL), lambda b, j, *_: (0, j, 0))
    return pl.BlockSpec((1, 1, D_MODEL), lambda b, j, *_: (b, 0, 0))


def _peer(x, sc, sh, ga, g, wpqT, keys16, u_bf, v_bf, *, t_tile, ei):
    B, S, D = x.shape
    per_row = sc.shape[1] != 1
    nt = S // t_tile
    ntot = B * S
    n_i = N_KEYS // ei
    col = lambda b, j, *_: (0, b * nt + j)
    col3 = lambda b, j, *_: (0, 0, b * nt + j)
    row3 = lambda b, j, *_: (b, j, 0)
    f = jax.ShapeDtypeStruct
    hT, s1, bb, cc, aa = pl.pallas_call(
        _peer_prep_kernel,
        grid=(B, nt),
        in_specs=[
            pl.BlockSpec((1, t_tile, D), row3),
            _mod_spec(per_row, t_tile), _mod_spec(per_row, t_tile),
            pl.BlockSpec((1, D), lambda b, j: (0, 0)),
            pl.BlockSpec((PEER_HEADS * PEER_QDIM, D), lambda b, j: (0, 0)),
            pl.BlockSpec((2 * PEER_HEADS, N_KEYS, N_KEYS), lambda b, j: (0, 0, 0)),
        ],
        out_specs=[
            pl.BlockSpec((D, t_tile), col),
            pl.BlockSpec((PEER_HEADS, N_KEYS, t_tile), col3),
            pl.BlockSpec((PEER_HEADS, N_KEYS, t_tile), col3),
            pl.BlockSpec((PEER_HEADS, N_KEYS, t_tile), col3),
            pl.BlockSpec((PEER_HEADS, N_KEYS, t_tile), col3),
        ],
        out_shape=[f((D, ntot), BF16)] + [f((PEER_HEADS, N_KEYS, ntot), F32)] * 4,
        scratch_shapes=[pltpu.VMEM((PEER_HEADS * PEER_QDIM, t_tile), BF16),
                        pltpu.VMEM((2, N_TOPV, PEER_HEADS, t_tile), F32)],
        compiler_params=_cparams(("parallel", "parallel")),
        name="peer_prep",
    )(x, sc, sh, g, wpqT, keys16)

    blk = ei * N_KEYS
    blk_b = lambda i: jnp.clip(i - 1, 0, n_i - 1)
    return pl.pallas_call(
        functools.partial(_peer_dense_kernel, n_i=n_i),
        grid=(B, nt, n_i + 2),
        in_specs=[
            pl.BlockSpec((D, t_tile), col),
            pl.BlockSpec((PEER_HEADS, N_KEYS, t_tile), col3),
            pl.BlockSpec((PEER_HEADS, N_KEYS, t_tile), col3),
            pl.BlockSpec((PEER_HEADS, ei, t_tile), lambda b, j, i: (0, blk_b(i), b * nt + j)),
            pl.BlockSpec((PEER_HEADS, ei, t_tile), lambda b, j, i: (0, blk_b(i), b * nt + j)),
            pl.BlockSpec((blk, D), lambda b, j, i: (jnp.minimum(i, n_i - 1), 0)),
            pl.BlockSpec((blk, D), lambda b, j, i: (jnp.clip(i - 2, 0, n_i - 1), 0)),
            pl.BlockSpec((1, t_tile, D), row3),
            _mod_spec(per_row, t_tile),
        ],
        out_specs=pl.BlockSpec((1, t_tile, D), row3),
        out_shape=f((B, S, D), F32),
        scratch_shapes=[pltpu.VMEM((t_tile, D), F32),
                        pltpu.VMEM((blk, t_tile), F32), pltpu.VMEM((blk, t_tile), F32),
                        pltpu.VMEM((blk, t_tile), BF16), pltpu.VMEM((blk, t_tile), BF16)],
        compiler_params=_cparams(("parallel", "parallel", "arbitrary")),
        name="peer_dense",
    )(hT, s1, bb, cc, aa, u_bf, v_bf, x, ga)


def _mm_kernel(*refs, n_in, prologue, has_bias, has_res):
    it = iter(refs)
    a_refs = [next(it) for _ in range(n_in)]
    if prologue == "normmod":
        g_ref, sc_ref, sh_ref = next(it), next(it), next(it)
    w_ref = next(it)
    b_ref = next(it) if has_bias else None
    if has_res:
        x_ref, ga_ref = next(it), next(it)
    o_ref, h_ref = next(it), next(it)

    @pl.when(pl.program_id(2) == 0)
    def _():
        a = a_refs[0][0]
        for r in a_refs[1:]:
            a = a + r[0]
        if prologue == "normmod":
            a = _normmod(a, g_ref[...], sc_ref[0], sh_ref[0])
        elif prologue == "silu":
            a = a / (1.0 + jnp.exp(-a))
        h_ref[...] = a.astype(BF16)

    y = jnp.dot(h_ref[...], w_ref[...], preferred_element_type=F32)
    if has_bias:
        y = y + b_ref[...]
    if has_res:
        y = x_ref[0] + ga_ref[0] * y
    o_ref[0] = y.astype(o_ref.dtype)


def _mm(a_list, w, *, tm, tn=None, norm=None, prologue=None, bias=None, res=None, name="mm"):
    B, S, K = a_list[0].shape
    N = w.shape[1]
    tn = N if tn is None else tn
    grid = (B, S // tm, N // tn)
    ins, specs = [], []
    for a in a_list:
        ins.append(a)
        specs.append(pl.BlockSpec((1, tm, K), lambda b, j, n: (b, j, 0)))

    def mod_spec(m, width, tiled_n):
        if m.shape[1] != 1:
            return pl.BlockSpec((1, tm, width), (lambda b, j, n: (0, j, n)) if tiled_n else (lambda b, j, n: (0, j, 0)))
        return pl.BlockSpec((1, 1, width), (lambda b, j, n: (b, 0, n)) if tiled_n else (lambda b, j, n: (b, 0, 0)))

    if norm is not None:
        g, sc, sh = norm
        prologue = "normmod"
        ins += [g, sc, sh]
        specs += [pl.BlockSpec((1, K), lambda b, j, n: (0, 0)), mod_spec(sc, K, False), mod_spec(sh, K, False)]
    ins.append(w)
    specs.append(pl.BlockSpec((K, tn), lambda b, j, n: (0, n)))
    if bias is not None:
        ins.append(bias)
        specs.append(pl.BlockSpec((1, tn), lambda b, j, n: (0, n)))
    if res is not None:
        x, ga = res
        ins += [x, ga]
        specs += [pl.BlockSpec((1, tm, tn), lambda b, j, n: (b, j, n)), mod_spec(ga, tn, True)]
    return pl.pallas_call(
        functools.partial(_mm_kernel, n_in=len(a_list), prologue=prologue,
                          has_bias=bias is not None, has_res=res is not None),
        grid=grid, in_specs=specs,
        out_specs=pl.BlockSpec((1, tm, tn), lambda b, j, n: (b, j, n)),
        out_shape=jax.ShapeDtypeStruct((B, S, N), F32),
        scratch_shapes=[pltpu.VMEM((tm, K), BF16)],
        compiler_params=_cparams(("parallel", "parallel", "arbitrary")),
        name=name,
    )(*ins)


def _rmsnorm_kernel(x_ref, g_ref, o_ref):
    x = x_ref[0]
    r = lax.rsqrt(jnp.mean(x * x, axis=-1, keepdims=True) + RMS_EPS)
    o_ref[0] = (x * r) * g_ref[...]


def _rmsnorm(x, g, *, tm):
    B, S, D = x.shape
    return pl.pallas_call(
        _rmsnorm_kernel, grid=(B, S // tm),
        in_specs=[pl.BlockSpec((1, tm, D), lambda b, j: (b, j, 0)), pl.BlockSpec((1, D), lambda b, j: (0, 0))],
        out_specs=pl.BlockSpec((1, tm, D), lambda b, j: (b, j, 0)),
        out_shape=jax.ShapeDtypeStruct((B, S, D), F32),
        compiler_params=_cparams(("parallel", "parallel")), name="final_norm",
    )(x, g)


def _chunk_proj(x_ref, w1_ref, n_chunks, p_ref):
    acc = None
    for s2 in range(CMP_STRIDE // 2):
        l0 = x_ref[pl.ds(2 * s2, n_chunks, stride=CMP_STRIDE), :]
        l1 = x_ref[pl.ds(2 * s2 + 1, n_chunks, stride=CMP_STRIDE), :]
        lhs = jnp.concatenate([l0, l1], axis=-1).astype(BF16)
        d = jnp.dot(lhs, w1_ref[0, s2], preferred_element_type=F32)
        acc = d if acc is None else acc + d
    p_ref[0:n_chunks, :] = acc


def _cmp_finish(p_ref, nb, b1_ref, w2T_ref, o_ref, g_base=0):
    for gp in range(2):
        c0 = gp * 2 * CMP_HID
        acc = p_ref[0:nb, c0:c0 + CMP_HID] + p_ref[1:nb + 1, c0 + CMP_HID:c0 + 2 * CMP_HID]
        hid = _gelu(acc + b1_ref[0]).astype(BF16)
        kT = lax.dot_general(w2T_ref[0], hid, (((1,), (1,)), ((), ())), preferred_element_type=F32)
        o_ref[0, 0, g_base + gp] = kT.astype(o_ref.dtype)


def _cmp_prompt_kernel(x_ref, w1_ref, b1_ref, w2T_ref, o_ref, p_ref, *, n_chunks, nbp):
    p_ref[n_chunks:, :] = jnp.zeros((p_ref.shape[0] - n_chunks, p_ref.shape[1]), F32)
    _chunk_proj(x_ref.at[0], w1_ref, n_chunks, p_ref)
    _cmp_finish(p_ref, nbp, b1_ref, w2T_ref, o_ref)


def _cmp_sample_kernel(pt_ref, *refs, n_pages, n_chunks):
    page_refs = refs[:n_pages]
    xn_ref, w1_ref, b1_ref, w2T_ref, o_ref, xs_ref, p_ref = refs[n_pages:]
    for gp in range(2):
        xs = xs_ref.at[gp]
        for p in range(n_pages):
            xs[p * PAGE_SIZE:(p + 1) * PAGE_SIZE, :] = page_refs[p][0, 0, gp].T
    for gp in range(2):
        pg = p_ref.at[gp]
        _chunk_proj(xs_ref.at[gp], w1_ref, n_chunks, pg)
        new = jnp.dot(xn_ref[0, :, gp * LANES:(gp + 1) * LANES].astype(BF16), w1_ref[0, 0, 0:LANES, :],
                      preferred_element_type=F32)
        pg[n_chunks:n_chunks + SUBLANES, :] = new
        _cmp_finish(pg, n_chunks, b1_ref, w2T_ref, o_ref, g_base=2 * gp)


def _cmp_weights(w1, w2):
    w1r = w1.reshape(2, 2, CMP_STRIDE, HEAD_DIM, CMP_HID).transpose(0, 2, 3, 1, 4)
    w1r = w1r.reshape(2, CMP_STRIDE, HEAD_DIM, 2 * CMP_HID)
    eye = jnp.eye(2, dtype=F32)
    wp = jnp.einsum("ab,ksdn->ksadbn", eye, w1r).reshape(2, CMP_STRIDE // 2, 4 * HEAD_DIM, 4 * CMP_HID)
    return wp.astype(BF16), w2.transpose(0, 2, 1).astype(BF16)


def _cmp_prompt(kv_c, w1p, b1, w2T):
    B, S, _ = kv_c.shape
    n_chunks = S // CMP_STRIDE
    nbp = n_chunks
    return pl.pallas_call(
        functools.partial(_cmp_prompt_kernel, n_chunks=n_chunks, nbp=nbp),
        grid=(B, 2, 2),
        in_specs=[
            pl.BlockSpec((1, S, LANES), lambda b, k, gp: (b, 0, k * 2 + gp)),
            pl.BlockSpec((1, CMP_STRIDE // 2, 2 * LANES, 4 * CMP_HID), lambda b, k, gp: (k, 0, 0, 0)),
            pl.BlockSpec((1, 1, CMP_HID), lambda b, k, gp: (k, 0, 0)),
            pl.BlockSpec((1, HEAD_DIM, CMP_HID), lambda b, k, gp: (k, 0, 0)),
        ],
        out_specs=pl.BlockSpec((1, 1, 2, HEAD_DIM, nbp), lambda b, k, gp: (b, k, gp, 0, 0)),
        out_shape=jax.ShapeDtypeStruct((B, 2, G_A, HEAD_DIM, nbp), BF16),
        scratch_shapes=[pltpu.VMEM((n_chunks + SUBLANES, 4 * CMP_HID), F32)],
        compiler_params=_cparams(("parallel", "parallel", "parallel")), name="cmp_prompt",
    )(kv_c, w1p, b1, w2T)


def _cmp_sample(cacheT, page_table, kv_c_new, w1p, b1, w2T):
    Bd, n_pages = page_table.shape
    n_chunks = n_pages * PAGE_SIZE // CMP_STRIDE

    def page_spec(p):
        return pl.BlockSpec((1, 1, 2, LANES, PAGE_SIZE), lambda b, k, pt: (pt[b, p], k, 0, 0, 0))

    return pl.pallas_call(
        functools.partial(_cmp_sample_kernel, n_pages=n_pages, n_chunks=n_chunks),
        grid_spec=pltpu.PrefetchScalarGridSpec(
            num_scalar_prefetch=1, grid=(Bd, 2),
            in_specs=[page_spec(p) for p in range(n_pages)] + [
                pl.BlockSpec((1, SUBLANES, 2 * LANES), lambda b, k, pt: (b, 0, k)),
                pl.BlockSpec((1, CMP_STRIDE // 2, 2 * LANES, 4 * CMP_HID), lambda b, k, pt: (k, 0, 0, 0)),
                pl.BlockSpec((1, 1, CMP_HID), lambda b, k, pt: (k, 0, 0)),
                pl.BlockSpec((1, HEAD_DIM, CMP_HID), lambda b, k, pt: (k, 0, 0)),
            ],
            out_specs=pl.BlockSpec((1, 1, G_A, HEAD_DIM, n_chunks), lambda b, k, pt: (b, k, 0, 0, 0)),
            scratch_shapes=[pltpu.VMEM((2, n_pages * PAGE_SIZE, LANES), F32),
                            pltpu.VMEM((2, n_chunks + SUBLANES, 4 * CMP_HID), F32)],
        ),
        out_shape=jax.ShapeDtypeStruct((Bd, 2, G_A, HEAD_DIM, n_chunks), BF16),
        compiler_params=_cparams(("parallel", "parallel")), name="cmp_sample",
    )(page_table, *([cacheT] * n_pages), kv_c_new, w1p, b1, w2T)


_LOWEST = -3.0e38


def _select_rounds(score, lane):
    picks = []
    for _ in range(N_SEL):
        m = jnp.max(score, axis=-1, keepdims=True)
        idx = jnp.min(jnp.where(score == m, lane, 1 << 20), axis=-1, keepdims=True)
        picks.append(idx)
        score = jnp.where(lane == idx, _LOWEST, score)
    return picks


def _overlap_matrix(nb_pad, ns_pad, nb, ns):
    cs = jnp.arange(nb_pad) * CMP_STRIDE
    ss = jnp.arange(ns_pad) * SEL_BLOCK
    ov = jnp.maximum(jnp.minimum(cs[:, None] + CMP_LEN, ss[None, :] + SEL_BLOCK)
                     - jnp.maximum(cs[:, None], ss[None, :]), 0).astype(F32) / CMP_LEN
    ov = jnp.where((jnp.arange(nb_pad)[:, None] < nb) & (jnp.arange(ns_pad)[None, :] < ns), ov, 0.0)
    return ov.astype(BF16)


def _row_info(rows, tq, q0):
    row = lax.broadcasted_iota(jnp.int32, (rows, 1), 0)
    return q0 + (row & (tq - 1))


def _cmp_attend_kernel(q_ref, kT_ref, vT_ref, slope_ref, gate_ref, ov_ref, o_ref, bits_ref, *, tq, nb, ns):
    R = q_ref.shape[2]
    rows = R * tq
    q0 = pl.program_id(2) * tq
    q = q_ref[0, 0].reshape(rows, HEAD_DIM)
    qpos = _row_info(rows, tq, q0)
    slope = slope_ref[0]
    nbp = kT_ref.shape[-1]
    blk_end = lax.broadcasted_iota(jnp.int32, (1, nbp), 1) * CMP_STRIDE + (CMP_LEN - 1)
    s = jnp.dot(q, kT_ref[0, 0, 0], preferred_element_type=F32)
    s = s + slope * (blk_end - q0).astype(F32)
    valid = (blk_end <= qpos) & (blk_end < nb * CMP_STRIDE + CMP_LEN - 1)
    s = jnp.where(valid, s, NEG)
    m = jnp.max(s, axis=-1, keepdims=True)
    e = jnp.exp(s - m)
    p = e * (jnp.where(qpos >= CMP_LEN - 1, 1.0, 0.0) / jnp.sum(e, axis=-1, keepdims=True))
    pb = p.astype(BF16)
    o = lax.dot_general(pb, vT_ref[0, 0, 0], (((1,), (1,)), ((), ())), preferred_element_type=F32)
    imp_r = jnp.dot(pb, ov_ref[...], preferred_element_type=F32)
    g = gate_ref[0, 0]
    imp = None
    for r in range(R):
        gate = 1.0 / (1.0 + jnp.exp(-g[:, r:r + 1]))
        o_ref[0, 0, r] = o[r * tq:(r + 1) * tq] * gate
        part = imp_r[r * tq:(r + 1) * tq]
        imp = part if imp is None else imp + part
    tpos = qpos[0:tq]
    lane = lax.broadcasted_iota(jnp.int32, (1, imp.shape[1]), 1)
    cur = tpos >> 6
    forced = (lane == 0) | (lane == cur) | (lane == cur - 1)
    score = jnp.where(lane * SEL_BLOCK <= tpos, imp + jnp.where(forced, FORCE, 0.0), NEG)
    score = jnp.where(lane < ns, score, _LOWEST)
    bits = jnp.zeros((tq, 1), jnp.int32)
    for idx in _select_rounds(score, lane):
        bits = bits | (1 << idx)
    bits_ref[0, 0] = jnp.broadcast_to(bits, (tq, LANES))


def _cmp_attend_prompt(q_t, kvcT, slope_rows, gates_t, ov, *, tq, nb, ns):
    B, G, R, S, _ = q_t.shape
    nbp = kvcT.shape[-1]
    return pl.pallas_call(
        functools.partial(_cmp_attend_kernel, tq=tq, nb=nb, ns=ns),
        grid=(B, G, S // tq),
        in_specs=[
            pl.BlockSpec((1, 1, R, tq, HEAD_DIM), lambda b, g, i: (b, g, 0, i, 0)),
            pl.BlockSpec((1, 1, 1, HEAD_DIM, nbp), lambda b, g, i: (b, 0, g, 0, 0)),
            pl.BlockSpec((1, 1, 1, HEAD_DIM, nbp), lambda b, g, i: (b, 1, g, 0, 0)),
            pl.BlockSpec((1, R * tq, 1), lambda b, g, i: (g, 0, 0)),
            pl.BlockSpec((1, 1, tq, 16), lambda b, g, i: (b, g, i, 0)),
            pl.BlockSpec(ov.shape, lambda b, g, i: (0, 0)),
        ],
        out_specs=[
            pl.BlockSpec((1, 1, R, tq, HEAD_DIM), lambda b, g, i: (b, g, 0, i, 0)),
            pl.BlockSpec((1, 1, tq, LANES), lambda b, g, i: (b, g, i, 0)),
        ],
        out_shape=[jax.ShapeDtypeStruct((B, G, R, S, HEAD_DIM), F32),
                   jax.ShapeDtypeStruct((B, G, S, LANES), jnp.int32)],
        compiler_params=_cparams(("parallel", "parallel", "parallel")), name="cmp_attend_prompt",
    )(q_t, kvcT, kvcT, slope_rows, gates_t, ov)


def _flash_kernel(*refs, tq, tk, window, use_bits, gate_col, use_sink):
    it = iter(refs)
    q_ref, kT_ref, v_ref, slope_ref = next(it), next(it), next(it), next(it)
    bits_ref = next(it) if use_bits else None
    gate_ref = next(it) if gate_col is not None else None
    sink_ref = next(it) if use_sink else None
    o_ref = next(it)
    m_ref, l_ref, acc_ref = next(it), next(it), next(it)
    R = q_ref.shape[2]
    qi = pl.program_id(2)
    q0 = qi * tq
    tpos = q0 + lax.broadcasted_iota(jnp.int32, (tq, 1), 0)
    if use_bits:
        bits = bits_ref[0, 0][:, 0:1]
    hi = (q0 + tq + tk - 1) // tk
    lo = 0 if window is None else jnp.maximum(q0 - (window - 1), 0) // tk
    m_ref[...] = jnp.full(m_ref.shape, NEG, F32)
    l_ref[...] = jnp.zeros(l_ref.shape, F32)
    acc_ref[...] = jnp.zeros(acc_ref.shape, F32)

    def body(j, carry):
        k0 = pl.multiple_of(j * tk, tk)
        kT = kT_ref[0, 0, :, pl.ds(k0, tk)]
        v = v_ref[0, 0, pl.ds(k0, tk), :]
        kpos = k0 + lax.broadcasted_iota(jnp.int32, (1, tk), 1)
        mask = kpos <= tpos
        if window is not None:
            mask = mask & (kpos > tpos - window)
        if use_bits:
            mask = mask & ((bits & (1 << (kpos >> 6))) != 0)
        mbias = jnp.where(mask, 0.0, NEG)
        kposf = (kpos - q0).astype(F32)
        hs = range(R)
        ss = [jnp.dot(q_ref[0, 0, r], kT, preferred_element_type=F32)
              + (mbias + slope_ref[0, r * tq:r * tq + 1, :] * kposf) for r in hs]
        m_olds = [m_ref[r] for r in hs]
        m_news = [jnp.maximum(m_olds[r], jnp.max(ss[r], axis=-1, keepdims=True)) for r in hs]
        alphas = [jnp.exp(m_olds[r] - m_news[r]) for r in hs]
        ps = [jnp.exp(ss[r] - m_news[r]) for r in hs]
        sums = [jnp.sum(ps[r], axis=-1, keepdims=True) for r in hs]
        pvs = [jnp.dot(ps[r].astype(BF16), v, preferred_element_type=F32) for r in hs]
        for r in hs:
            l_ref[r] = alphas[r] * l_ref[r] + sums[r]
            acc_ref[r] = alphas[r] * acc_ref[r] + pvs[r]
            m_ref[r] = m_news[r]
        return carry

    lax.fori_loop(lo, hi, body, 0)
    for r in range(R):
        m, l, acc = m_ref[r], l_ref[r], acc_ref[r]
        if use_sink:
            slope = slope_ref[0, r * tq:r * tq + 1, :]
            sk = sink_ref[0, r * tq:r * tq + 1, :] + slope * (tpos - q0).astype(F32)
            m_new = jnp.maximum(m, sk)
            alpha = jnp.exp(m - m_new)
            l = alpha * l + jnp.exp(sk - m_new)
            acc = alpha * acc
        orr = acc / l
        if gate_col is not None:
            c = gate_col * R + r
            orr = orr * (1.0 / (1.0 + jnp.exp(-gate_ref[0, 0][:, c:c + 1])))
        o_ref[0, 0, r] = orr


def _flash(q_t, kT, v, slope_rows, *, tq, tk, window=None, bits=None, gates_t=None, gate_col=None,
           sink_rows=None, name="flash"):
    B, G, R, S, _ = q_t.shape
    ins = [q_t, kT, v, slope_rows]
    specs = [
        pl.BlockSpec((1, 1, R, tq, HEAD_DIM), lambda b, g, i: (b, g, 0, i, 0)),
        pl.BlockSpec((1, 1, HEAD_DIM, S), lambda b, g, i: (b, g, 0, 0)),
        pl.BlockSpec((1, 1, S, HEAD_DIM), lambda b, g, i: (b, g, 0, 0)),
        pl.BlockSpec((1, R * tq, 1), lambda b, g, i: (g, 0, 0)),
    ]
    if bits is not None:
        ins.append(bits)
        specs.append(pl.BlockSpec((1, 1, tq, LANES), lambda b, g, i: (b, g, i, 0)))
    if gates_t is not None:
        ins.append(gates_t)
        specs.append(pl.BlockSpec((1, 1, tq, 16), lambda b, g, i: (b, g, i, 0)))
    if sink_rows is not None:
        ins.append(sink_rows)
        specs.append(pl.BlockSpec((1, R * tq, 1), lambda b, g, i: (g, 0, 0)))
    return pl.pallas_call(
        functools.partial(_flash_kernel, tq=tq, tk=tk, window=window, use_bits=bits is not None,
                          gate_col=gate_col if gates_t is not None else None, use_sink=sink_rows is not None),
        grid=(B, G, S // tq), in_specs=specs,
        out_specs=pl.BlockSpec((1, 1, R, tq, HEAD_DIM), lambda b, g, i: (b, g, 0, i, 0)),
        out_shape=jax.ShapeDtypeStruct((B, G, R, S, HEAD_DIM), F32),
        scratch_shapes=[pltpu.VMEM((R, tq, 1), F32), pltpu.VMEM((R, tq, 1), F32),
                        pltpu.VMEM((R, tq, HEAD_DIM), F32)],
        compiler_params=_cparams(("parallel", "parallel", "parallel")), name=name,
    )(*ins)


def _softmax_parts(s, valid, extra_s, extra_valid, sink):
    s = jnp.where(valid, s, NEG)
    m = jnp.max(s, axis=-1, keepdims=True)
    if extra_s is not None:
        extra_s = jnp.where(extra_valid, extra_s, NEG)
        m = jnp.maximum(m, extra_s)
    if sink is not None:
        m = jnp.maximum(m, sink)
    e = jnp.exp(s - m)
    l = jnp.sum(e, axis=-1, keepdims=True)
    e_new = None
    if extra_s is not None:
        e_new = jnp.where(extra_valid, jnp.exp(extra_s - m), 0.0)
        l = l + e_new
    if sink is not None:
        l = l + jnp.exp(sink - m)
    return e, e_new, l


def _new_key_score(q16, k_new):
    return jnp.sum(q16.astype(F32) * k_new.astype(BF16).astype(F32), axis=-1, keepdims=True)


def _dec_cmp_kernel(q_ref, kT_ref, vT_ref, slope_ref, gate_ref, ov_ref, o_ref, idx_ref, *, qpos, n_past_blocks):
    q16 = q_ref[0]
    slope = slope_ref[...]
    nb = kT_ref.shape[-1]
    nsp = ov_ref.shape[1]
    blk_end = lax.broadcasted_iota(jnp.int32, (1, nb), 1) * CMP_STRIDE + (CMP_LEN - 1)
    valid = blk_end <= qpos
    bias = slope * (blk_end - qpos).astype(F32)
    row = lax.broadcasted_iota(jnp.int32, (H_A, 1), 0)
    lane = lax.broadcasted_iota(jnp.int32, (1, nsp), 1)
    cur = qpos // SEL_BLOCK
    forced = (lane == 0) | (lane == cur) | (lane == cur - 1)
    ns = n_past_blocks + 1
    o = jnp.zeros((H_A, HEAD_DIM), F32)
    tile_l = lax.broadcasted_iota(jnp.int32, (SUBLANES, LANES), 1)
    row8 = lax.broadcasted_iota(jnp.int32, (SUBLANES, 1), 0)
    score8 = jnp.full((SUBLANES, nsp), _LOWEST, F32)
    gs = range(G_A)
    ss = [jnp.where(valid, jnp.dot(q16, kT_ref[0, 0, g], preferred_element_type=F32) + bias, NEG) for g in gs]
    ms = [jnp.max(ss[g], axis=-1, keepdims=True) for g in gs]
    es = [jnp.exp(ss[g] - ms[g]) for g in gs]
    ls = [jnp.sum(es[g], axis=-1, keepdims=True) for g in gs]
    pbs = [(es[g] / ls[g]).astype(BF16) for g in gs]
    ogs = [lax.dot_general(pbs[g], vT_ref[0, 0, g], (((1,), (1,)), ((), ())), preferred_element_type=F32)
           for g in gs]
    imps = [jnp.dot(pbs[g], ov_ref[...], preferred_element_type=F32) for g in gs]
    for g in gs:
        mine = (row >> 2) == g
        o = jnp.where(mine, ogs[g], o)
        imp = jnp.sum(jnp.where(mine, imps[g], 0.0), axis=0, keepdims=True)
        score = jnp.where(lane * SEL_BLOCK <= qpos, imp + jnp.where(forced, FORCE, 0.0), NEG)
        score = jnp.where(lane < ns, score, _LOWEST)
        score8 = jnp.where(row8 == g, score, score8)
    tile = jnp.zeros((SUBLANES, LANES), jnp.int32)
    for r, idx in enumerate(_select_rounds(score8, lane)):
        tile = jnp.where(tile_l == r, idx, tile)
    gate = 1.0 / (1.0 + jnp.exp(-gate_ref[0][:, 0:1]))
    o_ref[0] = o * gate
    idx_ref[0] = tile


def _dec_cmp(q16, kvcT, slope16, gates16, ov, *, qpos, n_past_blocks):
    Bd = q16.shape[0]
    nb = kvcT.shape[-1]
    return pl.pallas_call(
        functools.partial(_dec_cmp_kernel, qpos=qpos, n_past_blocks=n_past_blocks),
        grid=(Bd,),
        in_specs=[
            pl.BlockSpec((1, H_A, HEAD_DIM), lambda b: (b, 0, 0)),
            pl.BlockSpec((1, 1, G_A, HEAD_DIM, nb), lambda b: (b, 0, 0, 0, 0)),
            pl.BlockSpec((1, 1, G_A, HEAD_DIM, nb), lambda b: (b, 1, 0, 0, 0)),
            pl.BlockSpec((H_A, 1), lambda b: (0, 0)),
            pl.BlockSpec((1, H_A, 4), lambda b: (b, 0, 0)),
            pl.BlockSpec(ov.shape, lambda b: (0, 0)),
        ],
        out_specs=[pl.BlockSpec((1, H_A, HEAD_DIM), lambda b: (b, 0, 0)),
                   pl.BlockSpec((1, SUBLANES, LANES), lambda b: (b, 0, 0))],
        out_shape=[jax.ShapeDtypeStruct((Bd, H_A, HEAD_DIM), F32),
                   jax.ShapeDtypeStruct((Bd, SUBLANES, LANES), jnp.int32)],
        compiler_params=_cparams(("parallel",)), name="dec_cmp",
    )(q16, kvcT, kvcT, slope16, gates16, ov)


def _dec_sel_kernel(idx_ref, pt_ref, *refs, qpos, n_past_blocks):
    k_refs = refs[:N_SEL]
    v_refs = refs[N_SEL:2 * N_SEL]
    q_ref, kn_ref, vn_ref, slope_ref, gate_ref, o_ref = refs[2 * N_SEL:]
    b = pl.program_id(0)
    g = pl.program_id(1)
    q16 = q_ref[0]
    slope = slope_ref[...]
    lane = lax.broadcasted_iota(jnp.int32, (1, PAGE_SIZE), 1)
    s_l, valid_l, v_l = [], [], []
    any_new = None
    for kk in range(N_SEL):
        idx = idx_ref[(b * G_A + g) * N_SEL + kk]
        is_new = (jnp.zeros((H_A, 1), jnp.int32) + idx) >= n_past_blocks
        any_new = is_new if any_new is None else (any_new | is_new)
        jp = jnp.minimum(idx, n_past_blocks - 1)
        kpos = (jp >> 1) * PAGE_SIZE + lane
        s = jnp.dot(q16, k_refs[kk][0, 0, 0].astype(BF16), preferred_element_type=F32)
        s_l.append(s - slope * (qpos - kpos).astype(F32))
        in_block = ((lane >> 6) == (jp & 1)) & ((lane * 0 + idx) < n_past_blocks) & (kpos <= qpos)
        valid_l.append(jnp.broadcast_to(in_block, (H_A, PAGE_SIZE)))
        v_l.append(v_refs[kk][0, 0, 0].astype(BF16))
    s = jnp.concatenate(s_l, axis=-1)
    valid = jnp.concatenate(valid_l, axis=-1)
    vT = jnp.concatenate(v_l, axis=-1)
    s_new = _new_key_score(q16, kn_ref[0, 0, 0])
    e, e_new, l = _softmax_parts(s, valid, s_new, any_new, None)
    o = lax.dot_general(e.astype(BF16), vT, (((1,), (1,)), ((), ())), preferred_element_type=F32)
    o = o + e_new * vn_ref[0, 0, 0].astype(BF16).astype(F32)
    gate = 1.0 / (1.0 + jnp.exp(-gate_ref[0][:, 1:2]))
    o_ref[0, 0] = (o / l) * gate


def _dec_sel(idx_flat, page_table, cacheT, q16, kv_new, slope16, gates16, *, qpos, n_past_blocks):
    Bd = q16.shape[0]

    def tile_spec(kk, kind):
        def imap(b, g, idx, pt):
            jp = jnp.minimum(idx[(b * G_A + g) * N_SEL + kk], n_past_blocks - 1)
            return (pt[b, jp >> 1], kind, g, 0, 0)
        return pl.BlockSpec((1, 1, 1, HEAD_DIM, PAGE_SIZE), imap)

    return pl.pallas_call(
        functools.partial(_dec_sel_kernel, qpos=qpos, n_past_blocks=n_past_blocks),
        grid_spec=pltpu.PrefetchScalarGridSpec(
            num_scalar_prefetch=2, grid=(Bd, G_A),
            in_specs=[tile_spec(kk, 0) for kk in range(N_SEL)] + [tile_spec(kk, 1) for kk in range(N_SEL)] + [
                pl.BlockSpec((1, H_A, HEAD_DIM), lambda b, g, idx, pt: (b, 0, 0)),
                pl.BlockSpec((1, 1, 1, 1, HEAD_DIM), lambda b, g, idx, pt: (b, 0, g, 0, 0)),
                pl.BlockSpec((1, 1, 1, 1, HEAD_DIM), lambda b, g, idx, pt: (b, 1, g, 0, 0)),
                pl.BlockSpec((H_A, 1), lambda b, g, idx, pt: (0, 0)),
                pl.BlockSpec((1, H_A, 4), lambda b, g, idx, pt: (b, 0, 0)),
            ],
            out_specs=pl.BlockSpec((1, 1, H_A, HEAD_DIM), lambda b, g, idx, pt: (b, g, 0, 0)),
        ),
        out_shape=jax.ShapeDtypeStruct((Bd, G_A, H_A, HEAD_DIM), F32),
        compiler_params=_cparams(("parallel", "parallel")), name="dec_sel",
    )(idx_flat, page_table, *([cacheT] * (2 * N_SEL)), q16, kv_new, kv_new, slope16, gates16)


def _dec_slab_kernel(*refs, n_groups, window, qpos, gate_col, use_sink):
    it = iter(refs)
    q_ref, kT_ref, vT_ref, kn_ref, vn_ref, slope_ref = (next(it) for _ in range(6))
    gate_ref = next(it) if gate_col is not None else None
    sink_ref = next(it) if use_sink else None
    o_ref = next(it)
    q16 = q_ref[0]
    slope = slope_ref[...]
    nbuf = kT_ref.shape[-1]
    rpg = q16.shape[0] // n_groups
    kpos = qpos - nbuf + lax.broadcasted_iota(jnp.int32, (1, nbuf), 1)
    dist = qpos - kpos
    valid = (dist >= 0) & (dist < window) & (kpos >= 0)
    bias = -slope * dist.astype(F32)
    row = lax.broadcasted_iota(jnp.int32, (q16.shape[0], 1), 0)
    sink = sink_ref[...] if use_sink else None
    o = jnp.zeros((q16.shape[0], HEAD_DIM), F32)
    for g in range(n_groups):
        s = jnp.dot(q16, kT_ref[0, 0, g].astype(BF16), preferred_element_type=F32) + bias
        s_new = _new_key_score(q16, kn_ref[0, 0, g:g + 1, :])
        e, e_new, l = _softmax_parts(s, valid, s_new, True, sink)
        og = lax.dot_general(e.astype(BF16), vT_ref[0, 0, g].astype(BF16), (((1,), (1,)), ((), ())),
                             preferred_element_type=F32)
        og = (og + e_new * vn_ref[0, 0, g:g + 1, :].astype(BF16).astype(F32)) / l
        o = jnp.where((row // rpg) == g, og, o)
    if gate_col is not None:
        o = o * (1.0 / (1.0 + jnp.exp(-gate_ref[0][:, gate_col:gate_col + 1])))
    o_ref[0] = o


def _dec_slab(q16, slabT, kv_new, slope16, *, window, qpos, gates16=None, gate_col=None, sink16=None, name):
    Bd, _, G, _, nbuf = slabT.shape
    H = q16.shape[1]
    ins = [q16, slabT, slabT, kv_new, kv_new, slope16]
    specs = [
        pl.BlockSpec((1, H, HEAD_DIM), lambda b: (b, 0, 0)),
        pl.BlockSpec((1, 1, G, HEAD_DIM, nbuf), lambda b: (b, 0, 0, 0, 0)),
        pl.BlockSpec((1, 1, G, HEAD_DIM, nbuf), lambda b: (b, 1, 0, 0, 0)),
        pl.BlockSpec((1, 1, G, HEAD_DIM), lambda b: (b, 0, 0, 0)),
        pl.BlockSpec((1, 1, G, HEAD_DIM), lambda b: (b, 1, 0, 0)),
        pl.BlockSpec((H, 1), lambda b: (0, 0)),
    ]
    if gates16 is not None:
        ins.append(gates16)
        specs.append(pl.BlockSpec((1, H, 4), lambda b: (b, 0, 0)))
    if sink16 is not None:
        ins.append(sink16)
        specs.append(pl.BlockSpec((H, 1), lambda b: (0, 0)))
    return pl.pallas_call(
        functools.partial(_dec_slab_kernel, n_groups=G, window=window, qpos=qpos,
                          gate_col=gate_col if gates16 is not None else None, use_sink=sink16 is not None),
        grid=(Bd,), in_specs=specs,
        out_specs=pl.BlockSpec((1, H, HEAD_DIM), lambda b: (b, 0, 0)),
        out_shape=jax.ShapeDtypeStruct((Bd, H, HEAD_DIM), F32),
        compiler_params=_cparams(("parallel",)), name=name,
    )(*ins)


def _alibi(n_heads):
    return 2.0 ** (-8.0 * jnp.arange(1, n_heads + 1, dtype=F32) / n_heads)


def _heads_first(z, G, R):
    B, S, _ = z.shape
    return z.reshape(B, S, G, R, HEAD_DIM).transpose(0, 2, 3, 1, 4)


def _heads_last(o):
    B, G, R, S, _ = o.shape
    return o.transpose(0, 3, 1, 2, 4).reshape(B, S, G * R * HEAD_DIM)


def kernel(x_prompt, x_sample, c_prompt, c_sample, cache_cmp_kv, cache_sel_kv, state_win_kv, state_shared_kv, page_table, w_ada, b_ada, g_norm, w_in_a, w_o_a, cmp_pos, w_cmp1, b_cmp1, w_cmp2, g_kv, w_ada_kv, b_ada_kv, w_kv_b, w_q_b, w_o_b, sinks, w_pq, peer_keys, peer_u, peer_v, g_final):
    B, S, D = x_prompt.shape
    Bd = x_sample.shape[0]
    n_pages = page_table.shape[1]
    past_len = n_pages * PAGE_SIZE
    n_past_blocks = past_len // SEL_BLOCK
    TQ = 128
    TQ_CMP = 512
    bf = lambda a: a.astype(BF16)

    c_all = jnp.concatenate([c_prompt, c_sample], axis=0)[None]
    nc = B + Bd

    def ada(w, b):
        return _mm([c_all], bf(w), tm=nc, tn=2048, prologue="silu", bias=b[None], name="ada")[0]

    mods = [ada(w_ada[i], b_ada[i]) for i in range(2)]
    mods_kv = ada(w_ada_kv, b_ada_kv)

    def split_mods(m, n):
        parts = jnp.split(m, n, axis=-1)
        return [p[:B, None, :] for p in parts], [p[None, B:, :] for p in parts]

    mp0, ms0 = split_mods(mods[0], 6)
    mp1, ms1 = split_mods(mods[1], 6)
    mpk, msk = split_mods(mods_kv, 2)

    w_in = bf(jnp.pad(w_in_a[0], ((0, 0), (0, 21 * LANES - NSA_IN))))
    w1p, w2T = _cmp_weights(w_cmp1[0], w_cmp2[0])
    pos_rows = jnp.pad(cmp_pos[0].reshape(2, 1, CMP_LEN * HEAD_DIM), ((0, 0), (0, SUBLANES - 1), (0, 0)))
    b1 = jnp.stack([_mm([pos_rows[k:k + 1]], bf(w_cmp1[0, k]), tm=SUBLANES, bias=b_cmp1[0, k][None],
                        name="cmp_bias")[0, 0:1] for k in range(2)])
    slopes_a = _alibi(H_A)
    slopes_b = _alibi(H_B)
    gn = lambda v: v.reshape(1, D)
    peer_w = [(bf(w_pq[i].T), bf(peer_keys[i].reshape(2 * PEER_HEADS, N_KEYS, N_KEYS)), bf(peer_u[i]), bf(peer_v[i]))
              for i in range(2)]

    def rows_of(v, G, R, tq=TQ):
        return jnp.repeat(v.reshape(G, R), tq, axis=1)[..., None]

    z = _mm([x_prompt], w_in, tm=512, norm=(gn(g_norm[0, 0]), mp0[1], mp0[0]), name="nsa_in_p")
    kv_c, kv_s, kv_w = (z[..., NSA_Q + i * NSA_KV:NSA_Q + (i + 1) * NSA_KV] for i in range(3))
    q_t = bf(_heads_first(z[..., :NSA_Q] * (HEAD_DIM ** -0.5), G_A, R_A))
    glog = z[..., NSA_Q + 3 * NSA_KV:NSA_IN].reshape(B, S, 3, G_A, R_A).transpose(0, 3, 1, 2, 4)
    gates_t = jnp.pad(glog.reshape(B, G_A, S, 3 * R_A), ((0, 0), (0, 0), (0, 0), (0, 16 - 3 * R_A)))
    slope_rows_a = rows_of(slopes_a, G_A, R_A)

    kvcT = _cmp_prompt(kv_c, w1p, b1, w2T)
    nb_p = S // CMP_STRIDE - 1
    ns_p = S // SEL_BLOCK
    ov_p = _overlap_matrix(kvcT.shape[-1], LANES, nb_p, ns_p)
    o_c, bits = _cmp_attend_prompt(q_t, kvcT, rows_of(slopes_a, G_A, R_A, TQ_CMP), gates_t, ov_p,
                                   tq=TQ_CMP, nb=nb_p, ns=ns_p)

    def kT_v(kv, G):
        kv5 = kv.reshape(B, S, 2, G, HEAD_DIM)
        return bf(kv5[:, :, 0].transpose(0, 2, 3, 1)), bf(kv5[:, :, 1].transpose(0, 2, 1, 3))

    kT_s, v_s = kT_v(kv_s, G_A)
    o_s = _flash(q_t, kT_s, v_s, slope_rows_a, tq=TQ, tk=256, bits=bits, gates_t=gates_t, gate_col=1, name="sel_p")
    kT_w, v_w = kT_v(kv_w, G_A)
    o_w = _flash(q_t, kT_w, v_w, slope_rows_a, tq=TQ, tk=256, window=WIN_A, gates_t=gates_t, gate_col=2, name="win_p")
    x1 = _mm([_heads_last(o_c), _heads_last(o_s), _heads_last(o_w)], bf(w_o_a[0]), tm=512,
             res=(x_prompt, mp0[2]), name="nsa_out_p")
    x1 = _peer(x1, mp0[4], mp0[3], mp0[5], gn(g_norm[0, 1]), *peer_w[0], t_tile=512, ei=8)

    kv_sh = _mm([x1], bf(w_kv_b), tm=512, norm=(gn(g_kv), mpk[1], mpk[0]), name="kv_sh_p")
    q1 = _mm([x1], bf(w_q_b[0]), tm=512, norm=(gn(g_norm[1, 0]), mp1[1], mp1[0]), name="q_b_p")
    q1_t = bf(_heads_first(q1 * (HEAD_DIM ** -0.5), G_B, R_B))
    kT_b, v_b = kT_v(kv_sh, G_B)
    o_b = _flash(q1_t, kT_b, v_b, rows_of(slopes_b, G_B, R_B), tq=TQ, tk=128, window=WIN_B,
                 sink_rows=rows_of(sinks[0], G_B, R_B), name="swa_p")
    x2 = _mm([_heads_last(o_b)], bf(w_o_b[0]), tm=512, res=(x1, mp1[2]), name="swa_out_p")
    x2 = _peer(x2, mp1[4], mp1[3], mp1[5], gn(g_norm[1, 1]), *peer_w[1], t_tile=512, ei=8)
    y_prompt = _rmsnorm(x2, gn(g_final), tm=512)

    xs = x_sample.reshape(1, Bd, D)
    zs = _mm([xs], w_in, tm=Bd, norm=(gn(g_norm[0, 0]), ms0[1], ms0[0]), name="nsa_in_s")[0]
    kvs_new = [zs[:, NSA_Q + i * NSA_KV:NSA_Q + (i + 1) * NSA_KV] for i in range(3)]
    q16 = bf(zs[:, :NSA_Q] * (HEAD_DIM ** -0.5)).reshape(Bd, H_A, HEAD_DIM)
    gates16 = jnp.pad(zs[:, NSA_Q + 3 * NSA_KV:NSA_IN].reshape(Bd, 3, H_A).transpose(0, 2, 1), ((0, 0), (0, 0), (0, 1)))
    slope16_a = slopes_a.reshape(H_A, 1)

    cmpT = cache_cmp_kv[0].transpose(0, 2, 3, 4, 1)
    n_phys = cmpT.shape[0]
    kvc_new = jnp.pad(kvs_new[0][:, None, :], ((0, 0), (0, SUBLANES - 1), (0, 0)))
    kvcT_s = _cmp_sample(cmpT.reshape(n_phys, 2, 2, 2 * HEAD_DIM, PAGE_SIZE), page_table, kvc_new, w1p, b1, w2T)
    nb_s = kvcT_s.shape[-1] - 1
    ns_pad = 2 * LANES
    ov_s = _overlap_matrix(kvcT_s.shape[-1], ns_pad, nb_s + 1, n_past_blocks + 1)
    o_cs, idx_tile = _dec_cmp(q16, kvcT_s, slope16_a, gates16, ov_s, qpos=past_len, n_past_blocks=n_past_blocks)
    idx_flat = idx_tile[:, :G_A, :N_SEL].reshape(-1)

    selT = cache_sel_kv[0].transpose(0, 2, 3, 4, 1)
    kvs_new5 = kvs_new[1].reshape(Bd, 2, G_A, 1, HEAD_DIM)
    o_ss4 = _dec_sel(idx_flat, page_table, selT, q16, kvs_new5, slope16_a, gates16,
                     qpos=past_len, n_past_blocks=n_past_blocks)
    own = (jnp.arange(H_A) // R_A)[None, :] == jnp.arange(G_A)[:, None]
    o_ss = jnp.sum(jnp.where(own[None, :, :, None], o_ss4, 0.0), axis=1)

    winT = state_win_kv[0].transpose(0, 2, 3, 4, 1)
    o_ws = _dec_slab(q16, winT, kvs_new[2].reshape(Bd, 2, G_A, HEAD_DIM), slope16_a, window=WIN_A, qpos=past_len,
                     gates16=gates16, gate_col=2, name="dec_win")
    flat = lambda o: o.reshape(1, Bd, H_A * HEAD_DIM)
    xs1 = _mm([flat(o_cs), flat(o_ss), flat(o_ws)], bf(w_o_a[0]), tm=Bd, res=(xs, ms0[2]), name="nsa_out_s")
    xs1 = _peer(xs1, ms0[4], ms0[3], ms0[5], gn(g_norm[0, 1]), *peer_w[0], t_tile=Bd, ei=8)

    kv_sh_s = _mm([xs1], bf(w_kv_b), tm=Bd, norm=(gn(g_kv), msk[1], msk[0]), name="kv_sh_s")[0]
    q1s = _mm([xs1], bf(w_q_b[0]), tm=Bd, norm=(gn(g_norm[1, 0]), ms1[1], ms1[0]), name="q_b_s")[0]
    q16b = bf(q1s * (HEAD_DIM ** -0.5)).reshape(Bd, H_B, HEAD_DIM)
    shT = state_shared_kv.transpose(0, 2, 3, 4, 1)
    o_bs = _dec_slab(q16b, shT, kv_sh_s.reshape(Bd, 2, G_B, HEAD_DIM), slopes_b.reshape(H_B, 1), window=WIN_B,
                     qpos=past_len, sink16=sinks[0].reshape(H_B, 1), name="dec_swa")
    xs2 = _mm([flat(o_bs)], bf(w_o_b[0]), tm=Bd, res=(xs1, ms1[2]), name="swa_out_s")
    xs2 = _peer(xs2, ms1[4], ms1[3], ms1[5], gn(g_norm[1, 1]), *peer_w[1], t_tile=Bd, ei=8)
    y_sample = _rmsnorm(xs2, gn(g_final), tm=Bd).reshape(Bd, 1, D)

    as_kv = lambda a, n, G: a.reshape(n, -1, 2, G, HEAD_DIM)
    nw = min(WIN_A, S)
    nwb = min(WIN_B, S)
    new_cmp_prompt = as_kv(kv_c, B, G_A)[None]
    new_sel_prompt = as_kv(kv_s, B, G_A)[None]
    new_win_prompt = as_kv(kv_w, B, G_A)[None, :, S - nw:]
    new_cmp_sample = as_kv(kvs_new[0], Bd, G_A)[None]
    new_sel_sample = as_kv(kvs_new[1], Bd, G_A)[None]
    new_win_sample = jnp.concatenate([state_win_kv[0], as_kv(kvs_new[2], Bd, G_A)], axis=1)[None, :, 1:]
    new_shared_prompt = as_kv(kv_sh, B, G_B)[:, S - nwb:]
    new_shared_sample = jnp.concatenate([state_shared_kv, as_kv(kv_sh_s, Bd, G_B)], axis=1)[:, 1:]
    return (y_prompt, y_sample, new_cmp_prompt, new_cmp_sample, new_sel_prompt, new_sel_sample,
            new_win_prompt, new_win_sample, new_shared_prompt, new_shared_sample)
```

```python
import functools

import jax
import jax.numpy as jnp
from jax import lax
from jax.experimental import pallas as pl
from jax.experimental.pallas import tpu as pltpu

F32 = jnp.float32
BF16 = jnp.bfloat16

D_MODEL = 1024
HEAD_DIM = 64
PAGE_SIZE = 128
H_A = 16
G_A = 4
R_A = H_A // G_A
CMP_LEN = 32
CMP_STRIDE = 16
CMP_HID = 128
SEL_BLOCK = 64
N_SEL = 8
WIN_A = 512
NSA_Q = H_A * HEAD_DIM
NSA_KV = 2 * G_A * HEAD_DIM
NSA_IN = NSA_Q + 3 * NSA_KV + 3 * H_A
H_B = 16
G_B = 2
R_B = H_B // G_B
WIN_B = 128
PEER_HEADS = 8
PEER_QDIM = 256
N_KEYS = 128
PEER_TOPK = 16
RMS_EPS = 1e-6
NEG = -1e30
FORCE = 1e3

LANES = 128
SUBLANES = 8
VMEM_LIMIT = 56 * 1024 * 1024


def _cparams(sem):
    return pltpu.CompilerParams(dimension_semantics=sem, vmem_limit_bytes=VMEM_LIMIT)


_GELU_C0 = 0.7978845608028654
_GELU_C1 = 0.7978845608028654 * 0.044715


def _gelu(x):
    return x * (0.5 * (1.0 + jnp.tanh(0.7978845608028654 * (x + 0.044715 * (x * x * x)))))


def _normmod(x, g, sc, sh):
    r = lax.rsqrt(jnp.mean(x * x, axis=-1, keepdims=True) + RMS_EPS)
    return (x * r) * g * (1.0 + sc) + sh


N_TOPV = PEER_TOPK + 1
_PEER_JB = 32
_PAIRS = [(p, q) for p in range(N_TOPV) for q in range(N_TOPV) if (p + 1) * (q + 1) <= N_TOPV]


def _peer_prep_kernel(x_ref, sc_ref, sh_ref, g_ref, wpqT_ref, keys_ref,
                      hT_ref, s1_ref, b_ref, c_ref, a_ref, qT_ref, topv_ref):
    T = x_ref.shape[1]
    h = _normmod(x_ref[0], g_ref[...], sc_ref[0], sh_ref[0])
    hT = h.T.astype(BF16)
    hT_ref[...] = hT
    qT_ref[...] = jnp.dot(wpqT_ref[...], hT, preferred_element_type=F32).astype(BF16)

    halves = (c_ref, s1_ref)
    for hc in range(2 * PEER_HEADS):
        q = qT_ref[hc * N_KEYS:(hc + 1) * N_KEYS, :]
        halves[hc % 2][hc // 2] = jnp.dot(keys_ref[hc], q, preferred_element_type=F32)

    topv_ref[:, 0] = jnp.full((2, PEER_HEADS, T), jnp.inf, F32)

    def topk_round(r, carry):
        for hc in range(2 * PEER_HEADS):
            c, hh = hc % 2, hc // 2
            s = halves[c][hh]
            below = jnp.where(s < topv_ref[c, r, hh:hh + 1, :], s, -jnp.inf)
            topv_ref[c, r + 1, hh:hh + 1, :] = jnp.max(below, axis=0, keepdims=True)
        return carry

    lax.fori_loop(0, N_TOPV, topk_round, 0)

    for tc in range(T // LANES):
        sl = slice(tc * LANES, (tc + 1) * LANES)
        av = [topv_ref[0, p + 1, :, sl] for p in range(N_TOPV)]
        bv = [topv_ref[1, q + 1, :, sl] for q in range(N_TOPV)]
        cands = [av[p] + bv[q] for (p, q) in _PAIRS]
        top = av[0] + bv[0]
        work = list(cands)
        kth = None
        prev = None
        for r in range(N_TOPV):
            m = work[0]
            for w in work[1:]:
                m = jnp.maximum(m, w)
            prev, kth = kth, m
            work = [jnp.where(w >= m, -jnp.inf, w) for w in work]
        tau = 0.5 * (prev + kth)
        z = jnp.zeros_like(top)
        for cnd in cands:
            z = z + jnp.where(cnd >= tau, jnp.exp(cnd - top), 0.0)
        topv_ref[0, 0, :, sl] = tau
        topv_ref[1, 0, :, sl] = 1.0 / z

    for hh in range(PEER_HEADS):
        a1 = topv_ref[0, 1, hh:hh + 1, :]
        b1 = topv_ref[1, 1, hh:hh + 1, :]
        tau = topv_ref[0, 0, hh:hh + 1, :]
        rz = topv_ref[1, 0, hh:hh + 1, :]
        s0 = c_ref[hh]
        c_ref[hh] = tau - s0
        a_ref[hh] = jnp.exp(s0 - a1) * (0.5 * rz)
        b_ref[hh] = jnp.exp(s1_ref[hh] - b1)


def _peer_dense_kernel(hT_ref, s1_ref, b_ref, c_ref, a_ref, u_ref, v_ref, x_ref, ga_ref,
                       o_ref, acc_ref, sa_ref, sb_ref, wa_ref, wb_ref, *, n_i, mw):
    i = pl.program_id(2)
    T = hT_ref.shape[1]
    ei = u_ref.shape[0] // N_KEYS
    s_bufs = (sa_ref, sb_ref)
    w_bufs = (wa_ref, wb_ref)

    tw = min(T, 2 * LANES)
    n_tw = T // tw

    blk = ei * N_KEYS

    def stage_a(dst):
        def piece(k, m):
            s_bufs[dst][m * mw:(m + 1) * mw, k * tw:(k + 1) * tw] = jnp.dot(
                u_ref[m * mw:(m + 1) * mw, :], hT_ref[:, k * tw:(k + 1) * tw], preferred_element_type=F32)
        return [functools.partial(piece, k, m) for k in range(n_tw) for m in range(blk // mw)]

    def stage_b(src):
        s_ref, w_ref = s_bufs[src], w_bufs[src]

        def block(ii, jb):
            jr = slice(jb * _PEER_JB, (jb + 1) * _PEER_JB)
            rows = slice(ii * N_KEYS + jb * _PEER_JB, ii * N_KEYS + (jb + 1) * _PEER_JB)
            w = jnp.zeros((_PEER_JB, T), F32)
            for hh in range(PEER_HEADS):
                cth = c_ref[hh, ii:ii + 1, :]
                ath = a_ref[hh, ii:ii + 1, :]
                w = w + jnp.where(s1_ref[hh, jr, :] >= cth, b_ref[hh, jr, :], 0.0) * ath
            s = s_ref[rows, :]
            g2 = s * (1.0 + jnp.tanh(s * (_GELU_C0 + _GELU_C1 * (s * s))))
            w_ref[rows, :] = (w * g2).astype(BF16)
        return [functools.partial(block, ii, jb) for ii in range(ei) for jb in range(N_KEYS // _PEER_JB)]

    def stage_c(src):
        def piece(k, m):
            acc_ref[k * tw:(k + 1) * tw, m * mw:(m + 1) * mw] += lax.dot_general(
                w_bufs[src][:, k * tw:(k + 1) * tw], v_ref[:, m * mw:(m + 1) * mw], (((0,), (0,)), ((), ())),
                preferred_element_type=F32)
        return [functools.partial(piece, k, m) for k in range(n_tw) for m in range(acc_ref.shape[1] // mw)]

    def run_interleaved(mxu_pieces, vpu_blocks):
        n = len(mxu_pieces)
        per = -(-len(vpu_blocks) // n)
        for k, piece in enumerate(mxu_pieces):
            piece()
            for blk_fn in vpu_blocks[k * per:(k + 1) * per]:
                blk_fn()

    @pl.when(i == 0)
    def _():
        acc_ref[...] = jnp.zeros_like(acc_ref)
        wa_ref[...] = jnp.zeros_like(wa_ref)
        wb_ref[...] = jnp.zeros_like(wb_ref)
        run_interleaved(stage_a(0), [])

    steady = (i >= 1) & (i <= n_i)
    for par in range(2):
        @pl.when(steady & (i % 2 == par))
        def _(par=par):
            run_interleaved(stage_a(par) + stage_c(par), stage_b(1 - par))

    @pl.when(i == n_i + 1)
    def _():
        run_interleaved(stage_c((n_i + 1) % 2), [])
        o_ref[0] = x_ref[0] + ga_ref[0] * acc_ref[...]


def _mod_spec(per_row, t):
    if per_row:
        return pl.BlockSpec((1, t, D_MODEL), lambda b, j, *_: (0, j, 0))
    return pl.BlockSpec((1, 1, D_MODEL), lambda b, j, *_: (b, 0, 0))


def _peer(x, sc, sh, ga, g, wpqT, keys16, u_bf, v_bf, *, t_tile, ei, mw=8 * LANES):
    B, S, D = x.shape
    per_row = sc.shape[1] != 1
    nt = S // t_tile
    ntot = B * S
    n_i = N_KEYS // ei
    col = lambda b, j, *_: (0, b * nt + j)
    col3 = lambda b, j, *_: (0, 0, b * nt + j)
    row3 = lambda b, j, *_: (b, j, 0)
    f = jax.ShapeDtypeStruct
    hT, s1, bb, cc, aa = pl.pallas_call(
        _peer_prep_kernel,
        grid=(B, nt),
        in_specs=[
            pl.BlockSpec((1, t_tile, D), row3),
            _mod_spec(per_row, t_tile), _mod_spec(per_row, t_tile),
            pl.BlockSpec((1, D), lambda b, j: (0, 0)),
            pl.BlockSpec((PEER_HEADS * PEER_QDIM, D), lambda b, j: (0, 0)),
            pl.BlockSpec((2 * PEER_HEADS, N_KEYS, N_KEYS), lambda b, j: (0, 0, 0)),
        ],
        out_specs=[
            pl.BlockSpec((D, t_tile), col),
            pl.BlockSpec((PEER_HEADS, N_KEYS, t_tile), col3),
            pl.BlockSpec((PEER_HEADS, N_KEYS, t_tile), col3),
            pl.BlockSpec((PEER_HEADS, N_KEYS, t_tile), col3),
            pl.BlockSpec((PEER_HEADS, N_KEYS, t_tile), col3),
        ],
        out_shape=[f((D, ntot), BF16)] + [f((PEER_HEADS, N_KEYS, ntot), F32)] * 4,
        scratch_shapes=[pltpu.VMEM((PEER_HEADS * PEER_QDIM, t_tile), BF16),
                        pltpu.VMEM((2, N_TOPV + 1, PEER_HEADS, t_tile), F32)],
        compiler_params=_cparams(("parallel", "parallel")),
        name="peer_prep",
    )(x, sc, sh, g, wpqT, keys16)

    blk = ei * N_KEYS
    blk_a = lambda i: jnp.minimum(i, n_i - 1)
    blk_b = lambda i: jnp.clip(i - 1, 0, n_i - 1)
    blk_c = lambda i: jnp.clip(i - 2, 0, n_i - 1)
    return pl.pallas_call(
        functools.partial(_peer_dense_kernel, n_i=n_i, mw=mw),
        grid=(B, nt, n_i + 2),
        in_specs=[
            pl.BlockSpec((D, t_tile), col),
            pl.BlockSpec((PEER_HEADS, N_KEYS, t_tile), col3),
            pl.BlockSpec((PEER_HEADS, N_KEYS, t_tile), col3),
            pl.BlockSpec((PEER_HEADS, ei, t_tile), lambda b, j, i: (0, blk_b(i), b * nt + j)),
            pl.BlockSpec((PEER_HEADS, ei, t_tile), lambda b, j, i: (0, blk_b(i), b * nt + j)),
            pl.BlockSpec((blk, D), lambda b, j, i: (blk_a(i), 0)),
            pl.BlockSpec((blk, D), lambda b, j, i: (blk_c(i), 0)),
            pl.BlockSpec((1, t_tile, D), row3),
            _mod_spec(per_row, t_tile),
        ],
        out_specs=pl.BlockSpec((1, t_tile, D), row3),
        out_shape=f((B, S, D), F32),
        scratch_shapes=[pltpu.VMEM((t_tile, D), F32),
                        pltpu.VMEM((blk, t_tile), F32), pltpu.VMEM((blk, t_tile), F32),
                        pltpu.VMEM((blk, t_tile), BF16), pltpu.VMEM((blk, t_tile), BF16)],
        compiler_params=_cparams(("parallel", "parallel", "arbitrary")),
        name="peer_dense",
    )(hT, s1, bb, cc, aa, u_bf, v_bf, x, ga)


def _mm_kernel(*refs, n_in, prologue, has_bias, has_res):
    it = iter(refs)
    a_refs = [next(it) for _ in range(n_in)]
    if prologue == "normmod":
        g_ref, sc_ref, sh_ref = next(it), next(it), next(it)
    w_ref = next(it)
    b_ref = next(it) if has_bias else None
    if has_res:
        x_ref, ga_ref = next(it), next(it)
    o_ref, h_ref = next(it), next(it)

    @pl.when(pl.program_id(2) == 0)
    def _():
        a = a_refs[0][0]
        for r in a_refs[1:]:
            a = a + r[0]
        if prologue == "normmod":
            a = _normmod(a, g_ref[...], sc_ref[0], sh_ref[0])
        elif prologue == "silu":
            a = a / (1.0 + jnp.exp(-a))
        h_ref[...] = a.astype(BF16)

    y = jnp.dot(h_ref[...], w_ref[...], preferred_element_type=F32)
    if has_bias:
        y = y + b_ref[...]
    if has_res:
        y = x_ref[0] + ga_ref[0] * y
    o_ref[0] = y.astype(o_ref.dtype)


def _mm(a_list, w, *, tm, tn=None, norm=None, prologue=None, bias=None, res=None, name="mm"):
    B, S, K = a_list[0].shape
    N = w.shape[1]
    tn = N if tn is None else tn
    grid = (B, S // tm, N // tn)
    ins, specs = [], []
    for a in a_list:
        ins.append(a)
        specs.append(pl.BlockSpec((1, tm, K), lambda b, j, n: (b, j, 0)))

    def mod_spec(m, width, tiled_n):
        if m.shape[1] != 1:
            return pl.BlockSpec((1, tm, width), (lambda b, j, n: (0, j, n)) if tiled_n else (lambda b, j, n: (0, j, 0)))
        return pl.BlockSpec((1, 1, width), (lambda b, j, n: (b, 0, n)) if tiled_n else (lambda b, j, n: (b, 0, 0)))

    if norm is not None:
        g, sc, sh = norm
        prologue = "normmod"
        ins += [g, sc, sh]
        specs += [pl.BlockSpec((1, K), lambda b, j, n: (0, 0)), mod_spec(sc, K, False), mod_spec(sh, K, False)]
    ins.append(w)
    specs.append(pl.BlockSpec((K, tn), lambda b, j, n: (0, n)))
    if bias is not None:
        ins.append(bias)
        specs.append(pl.BlockSpec((1, tn), lambda b, j, n: (0, n)))
    if res is not None:
        x, ga = res
        ins += [x, ga]
        specs += [pl.BlockSpec((1, tm, tn), lambda b, j, n: (b, j, n)), mod_spec(ga, tn, True)]
    return pl.pallas_call(
        functools.partial(_mm_kernel, n_in=len(a_list), prologue=prologue,
                          has_bias=bias is not None, has_res=res is not None),
        grid=grid, in_specs=specs,
        out_specs=pl.BlockSpec((1, tm, tn), lambda b, j, n: (b, j, n)),
        out_shape=jax.ShapeDtypeStruct((B, S, N), F32),
        scratch_shapes=[pltpu.VMEM((tm, K), BF16)],
        compiler_params=_cparams(("parallel", "parallel", "arbitrary")),
        name=name,
    )(*ins)


def _rmsnorm_kernel(x_ref, g_ref, o_ref):
    x = x_ref[0]
    r = lax.rsqrt(jnp.mean(x * x, axis=-1, keepdims=True) + RMS_EPS)
    o_ref[0] = (x * r) * g_ref[...]


def _rmsnorm(x, g, *, tm):
    B, S, D = x.shape
    return pl.pallas_call(
        _rmsnorm_kernel, grid=(B, S // tm),
        in_specs=[pl.BlockSpec((1, tm, D), lambda b, j: (b, j, 0)), pl.BlockSpec((1, D), lambda b, j: (0, 0))],
        out_specs=pl.BlockSpec((1, tm, D), lambda b, j: (b, j, 0)),
        out_shape=jax.ShapeDtypeStruct((B, S, D), F32),
        compiler_params=_cparams(("parallel", "parallel")), name="final_norm",
    )(x, g)


def _chunk_proj_steps(x_ref, w1_ref, n_chunks, p_ref):
    n_steps = CMP_STRIDE // 2
    state = {}

    def step(s2):
        l0 = x_ref[pl.ds(2 * s2, n_chunks, stride=CMP_STRIDE), :]
        l1 = x_ref[pl.ds(2 * s2 + 1, n_chunks, stride=CMP_STRIDE), :]
        lhs = jnp.concatenate([l0, l1], axis=-1).astype(BF16)
        d = jnp.dot(lhs, w1_ref[0, s2], preferred_element_type=F32)
        state["acc"] = d if s2 == 0 else state["acc"] + d
        if s2 == n_steps - 1:
            p_ref[0:n_chunks, :] = state.pop("acc")
    return [functools.partial(step, s2) for s2 in range(n_steps)]


def _chunk_proj(x_ref, w1_ref, n_chunks, p_ref):
    for step in _chunk_proj_steps(x_ref, w1_ref, n_chunks, p_ref):
        step()


def _cmp_finish(p_ref, nb, b1_ref, w2T_ref, o_ref, g_base=0):
    for gp in range(2):
        c0 = gp * 2 * CMP_HID
        acc = p_ref[0:nb, c0:c0 + CMP_HID] + p_ref[1:nb + 1, c0 + CMP_HID:c0 + 2 * CMP_HID]
        hid = _gelu(acc + b1_ref[0]).astype(BF16)
        kT = lax.dot_general(w2T_ref[0], hid, (((1,), (1,)), ((), ())), preferred_element_type=F32)
        o_ref[0, 0, g_base + gp] = kT.astype(o_ref.dtype)


def _cmp_prompt_kernel(x_ref, w1_ref, b1_ref, w2T_ref, o_ref, p_ref, *, n_chunks, nbp):
    p_ref[n_chunks:, :] = jnp.zeros((p_ref.shape[0] - n_chunks, p_ref.shape[1]), F32)
    _chunk_proj(x_ref.at[0], w1_ref, n_chunks, p_ref)
    _cmp_finish(p_ref, nbp, b1_ref, w2T_ref, o_ref)


def _cmp_sample_kernel(pt_ref, *refs, n_pages, n_chunks):
    page_refs = refs[:n_pages]
    xn_ref, w1_ref, b1_ref, w2T_ref, o_ref, xs_ref, p_ref = refs[n_pages:]

    def transposes(gp):
        per = max(1, n_pages * 2 // CMP_STRIDE)

        def some(p0):
            for p in range(p0, min(p0 + per, n_pages)):
                xs_ref[gp, p * PAGE_SIZE:(p + 1) * PAGE_SIZE, :] = page_refs[p][0, 0, gp].T
        return [functools.partial(some, p0) for p0 in range(0, n_pages, per)]

    def finish(gp):
        pg = p_ref.at[gp]
        new = jnp.dot(xn_ref[0, :, gp * LANES:(gp + 1) * LANES].astype(BF16), w1_ref[0, 0, 0:LANES, :],
                      preferred_element_type=F32)
        pg[n_chunks:n_chunks + SUBLANES, :] = new
        _cmp_finish(pg, n_chunks, b1_ref, w2T_ref, o_ref, g_base=2 * gp)

    for t in transposes(0):
        t()
    t1 = transposes(1)
    p0 = _chunk_proj_steps(xs_ref.at[0], w1_ref, n_chunks, p_ref.at[0])
    for k in range(max(len(t1), len(p0))):
        if k < len(t1):
            t1[k]()
        if k < len(p0):
            p0[k]()
    finish(0)
    _chunk_proj(xs_ref.at[1], w1_ref, n_chunks, p_ref.at[1])
    finish(1)


def _cmp_weights(w1, w2):
    w1r = w1.reshape(2, 2, CMP_STRIDE, HEAD_DIM, CMP_HID).transpose(0, 2, 3, 1, 4)
    w1r = w1r.reshape(2, CMP_STRIDE, HEAD_DIM, 2 * CMP_HID)
    eye = jnp.eye(2, dtype=F32)
    wp = jnp.einsum("ab,ksdn->ksadbn", eye, w1r).reshape(2, CMP_STRIDE // 2, 4 * HEAD_DIM, 4 * CMP_HID)
    return wp.astype(BF16), w2.transpose(0, 2, 1).astype(BF16)


def _cmp_prompt(kv_c, w1p, b1, w2T):
    B, S, _ = kv_c.shape
    n_chunks = S // CMP_STRIDE
    nbp = n_chunks
    return pl.pallas_call(
        functools.partial(_cmp_prompt_kernel, n_chunks=n_chunks, nbp=nbp),
        grid=(B, 2, 2),
        in_specs=[
            pl.BlockSpec((1, S, LANES), lambda b, k, gp: (b, 0, k * 2 + gp)),
            pl.BlockSpec((1, CMP_STRIDE // 2, 2 * LANES, 4 * CMP_HID), lambda b, k, gp: (k, 0, 0, 0)),
            pl.BlockSpec((1, 1, CMP_HID), lambda b, k, gp: (k, 0, 0)),
            pl.BlockSpec((1, HEAD_DIM, CMP_HID), lambda b, k, gp: (k, 0, 0)),
        ],
        out_specs=pl.BlockSpec((1, 1, 2, HEAD_DIM, nbp), lambda b, k, gp: (b, k, gp, 0, 0)),
        out_shape=jax.ShapeDtypeStruct((B, 2, G_A, HEAD_DIM, nbp), BF16),
        scratch_shapes=[pltpu.VMEM((n_chunks + SUBLANES, 4 * CMP_HID), F32)],
        compiler_params=_cparams(("parallel", "parallel", "parallel")), name="cmp_prompt",
    )(kv_c, w1p, b1, w2T)


def _cmp_sample(cacheT, page_table, kv_c_new, w1p, b1, w2T):
    Bd, n_pages = page_table.shape
    n_chunks = n_pages * PAGE_SIZE // CMP_STRIDE

    def page_spec(p):
        return pl.BlockSpec((1, 1, 2, LANES, PAGE_SIZE), lambda b, k, pt: (pt[b, p], k, 0, 0, 0))

    return pl.pallas_call(
        functools.partial(_cmp_sample_kernel, n_pages=n_pages, n_chunks=n_chunks),
        grid_spec=pltpu.PrefetchScalarGridSpec(
            num_scalar_prefetch=1, grid=(Bd, 2),
            in_specs=[page_spec(p) for p in range(n_pages)] + [
                pl.BlockSpec((1, SUBLANES, 2 * LANES), lambda b, k, pt: (b, 0, k)),
                pl.BlockSpec((1, CMP_STRIDE // 2, 2 * LANES, 4 * CMP_HID), lambda b, k, pt: (k, 0, 0, 0)),
                pl.BlockSpec((1, 1, CMP_HID), lambda b, k, pt: (k, 0, 0)),
                pl.BlockSpec((1, HEAD_DIM, CMP_HID), lambda b, k, pt: (k, 0, 0)),
            ],
            out_specs=pl.BlockSpec((1, 1, G_A, HEAD_DIM, n_chunks), lambda b, k, pt: (b, k, 0, 0, 0)),
            scratch_shapes=[pltpu.VMEM((2, n_pages * PAGE_SIZE, LANES), F32),
                            pltpu.VMEM((2, n_chunks + SUBLANES, 4 * CMP_HID), F32)],
        ),
        out_shape=jax.ShapeDtypeStruct((Bd, 2, G_A, HEAD_DIM, n_chunks), BF16),
        compiler_params=_cparams(("parallel", "parallel")), name="cmp_sample",
    )(page_table, *([cacheT] * n_pages), kv_c_new, w1p, b1, w2T)


_LOWEST = -3.0e38


def _select_rounds(score, lane):
    picks = []
    for _ in range(N_SEL):
        m = jnp.max(score, axis=-1, keepdims=True)
        idx = jnp.min(jnp.where(score == m, lane, 1 << 20), axis=-1, keepdims=True)
        picks.append(idx)
        score = jnp.where(lane == idx, _LOWEST, score)
    return picks


def _overlap_matrix(nb_pad, ns_pad, nb, ns):
    cs = jnp.arange(nb_pad) * CMP_STRIDE
    ss = jnp.arange(ns_pad) * SEL_BLOCK
    ov = jnp.maximum(jnp.minimum(cs[:, None] + CMP_LEN, ss[None, :] + SEL_BLOCK)
                     - jnp.maximum(cs[:, None], ss[None, :]), 0).astype(F32) / CMP_LEN
    ov = jnp.where((jnp.arange(nb_pad)[:, None] < nb) & (jnp.arange(ns_pad)[None, :] < ns), ov, 0.0)
    return ov.astype(BF16)


def _row_info(rows, tq, q0):
    row = lax.broadcasted_iota(jnp.int32, (rows, 1), 0)
    return q0 + (row & (tq - 1))


def _cmp_attend_kernel(q_ref, kT_ref, vT_ref, slope_ref, gate_ref, ov_ref, o_ref, bits_ref, *, tq, nb, ns):
    R = q_ref.shape[2]
    rows = R * tq
    q0 = pl.program_id(2) * tq
    q = q_ref[0, 0].reshape(rows, HEAD_DIM)
    qpos = _row_info(rows, tq, q0)
    slope = slope_ref[0]
    nbp = kT_ref.shape[-1]
    blk_end = lax.broadcasted_iota(jnp.int32, (1, nbp), 1) * CMP_STRIDE + (CMP_LEN - 1)
    s = jnp.dot(q, kT_ref[0, 0, 0], preferred_element_type=F32)
    s = s + slope * (blk_end - q0).astype(F32)
    valid = (blk_end <= qpos) & (blk_end < nb * CMP_STRIDE + CMP_LEN - 1)
    s = jnp.where(valid, s, NEG)
    m = jnp.max(s, axis=-1, keepdims=True)
    e = jnp.exp(s - m)
    p = e * (jnp.where(qpos >= CMP_LEN - 1, 1.0, 0.0) / jnp.sum(e, axis=-1, keepdims=True))
    pb = p.astype(BF16)
    o = lax.dot_general(pb, vT_ref[0, 0, 0], (((1,), (1,)), ((), ())), preferred_element_type=F32)
    imp_r = jnp.dot(pb, ov_ref[...], preferred_element_type=F32)
    g = gate_ref[0, 0]
    imp = None
    for r in range(R):
        gate = 1.0 / (1.0 + jnp.exp(-g[:, r:r + 1]))
        o_ref[0, 0, r] = o[r * tq:(r + 1) * tq] * gate
        part = imp_r[r * tq:(r + 1) * tq]
        imp = part if imp is None else imp + part
    tpos = qpos[0:tq]
    lane = lax.broadcasted_iota(jnp.int32, (1, imp.shape[1]), 1)
    cur = tpos >> 6
    forced = (lane == 0) | (lane == cur) | (lane == cur - 1)
    score = jnp.where(lane * SEL_BLOCK <= tpos, imp + jnp.where(forced, FORCE, 0.0), NEG)
    score = jnp.where(lane < ns, score, _LOWEST)
    bits = jnp.zeros((tq, 1), jnp.int32)
    for idx in _select_rounds(score, lane):
        bits = bits | (1 << idx)
    bits_ref[0, 0] = jnp.transpose(jnp.broadcast_to(bits, (tq, LANES)))[0:SUBLANES, :]


def _cmp_attend_prompt(q_t, kvcT, slope_rows, gates_t, ov, *, tq, nb, ns):
    B, G, R, S, _ = q_t.shape
    nbp = kvcT.shape[-1]
    return pl.pallas_call(
        functools.partial(_cmp_attend_kernel, tq=tq, nb=nb, ns=ns),
        grid=(B, G, S // tq),
        in_specs=[
            pl.BlockSpec((1, 1, R, tq, HEAD_DIM), lambda b, g, i: (b, g, 0, i, 0)),
            pl.BlockSpec((1, 1, 1, HEAD_DIM, nbp), lambda b, g, i: (b, 0, g, 0, 0)),
            pl.BlockSpec((1, 1, 1, HEAD_DIM, nbp), lambda b, g, i: (b, 1, g, 0, 0)),
            pl.BlockSpec((1, R * tq, 1), lambda b, g, i: (g, 0, 0)),
            pl.BlockSpec((1, 1, tq, 16), lambda b, g, i: (b, g, i, 0)),
            pl.BlockSpec(ov.shape, lambda b, g, i: (0, 0)),
        ],
        out_specs=[
            pl.BlockSpec((1, 1, R, tq, HEAD_DIM), lambda b, g, i: (b, g, 0, i, 0)),
            pl.BlockSpec((1, 1, SUBLANES, tq), lambda b, g, i: (b, g, 0, i)),
        ],
        out_shape=[jax.ShapeDtypeStruct((B, G, R, S, HEAD_DIM), F32),
                   jax.ShapeDtypeStruct((B, G, SUBLANES, S), jnp.int32)],
        compiler_params=_cparams(("parallel", "parallel", "parallel")), name="cmp_attend_prompt",
    )(q_t, kvcT, kvcT, slope_rows, gates_t, ov)


_QK_FEAT = 16


def _flash_kernel(*refs, tq, tk, window, use_bits, gate_row, use_sink):
    it = iter(refs)
    qT_ref, k_ref, vT_ref = next(it), next(it), next(it)
    bits_ref = next(it) if use_bits else None
    gate_ref = next(it) if gate_row is not None else None
    if use_sink:
        sink_ref, slope_ref = next(it), next(it)
    o_ref = next(it)
    m_ref, l_ref, acc_ref = next(it), next(it), next(it)
    R = qT_ref.shape[2]
    qi = pl.program_id(2)
    q0 = qi * tq
    tpos = q0 + lax.broadcasted_iota(jnp.int32, (1, tq), 1)
    if use_bits:
        bits = bits_ref[0, 0, 0:1, :]
    hi = (q0 + tq + tk - 1) // tk
    lo = 0 if window is None else jnp.maximum(q0 - (window - 1), 0) // tk
    m_ref[...] = jnp.full(m_ref.shape, NEG, F32)
    l_ref[...] = jnp.zeros(l_ref.shape, F32)
    acc_ref[...] = jnp.zeros(acc_ref.shape, F32)

    def body(j, carry):
        k0 = pl.multiple_of(j * tk, tk)
        kt = k_ref[0, 0, pl.ds(k0, tk), :]
        vT = vT_ref[0, 0, :, pl.ds(k0, tk)]
        kpos = k0 + lax.broadcasted_iota(jnp.int32, (tk, 1), 0)
        mask = kpos <= tpos
        if window is not None:
            mask = mask & (kpos > tpos - window)
        if use_bits:
            mask = mask & ((bits & (1 << (kpos >> 6))) != 0)
        mbias = jnp.where(mask, 0.0, NEG)
        hs = range(R)
        ss = [jnp.dot(kt, qT_ref[0, 0, r], preferred_element_type=F32) + mbias for r in hs]
        m_olds = [m_ref[r] for r in hs]
        m_news = [jnp.maximum(m_olds[r], jnp.max(ss[r], axis=0, keepdims=True)) for r in hs]
        alphas = [jnp.exp(m_olds[r] - m_news[r]) for r in hs]
        ps = [jnp.exp(ss[r] - m_news[r]) for r in hs]
        sums = [jnp.sum(ps[r], axis=0, keepdims=True) for r in hs]
        pvs = [jnp.dot(vT, ps[r].astype(BF16), preferred_element_type=F32) for r in hs]
        for r in hs:
            l_ref[r] = alphas[r] * l_ref[r] + sums[r]
            acc_ref[r] = alphas[r] * acc_ref[r] + pvs[r]
            m_ref[r] = m_news[r]
        return carry

    lax.fori_loop(lo, hi, body, 0)
    for r in range(R):
        m, l, acc = m_ref[r], l_ref[r], acc_ref[r]
        if use_sink:
            sk = sink_ref[0, r] + slope_ref[0, r] * tpos.astype(F32)
            m_new = jnp.maximum(m, sk)
            alpha = jnp.exp(m - m_new)
            l = alpha * l + jnp.exp(sk - m_new)
            acc = alpha * acc
        scale = 1.0 / l
        if gate_row is not None:
            c = gate_row * R + r
            scale = scale * (1.0 / (1.0 + jnp.exp(-gate_ref[0, 0, c:c + 1, :])))
        o_ref[0, 0, r] = acc * scale


def _flash(qT, k, vT, *, tq, tk, window=None, bits=None, gates_t=None, gate_row=None,
           sink_t=None, slope_t=None, name="flash"):
    B, G, R, KD, S = qT.shape
    ins = [qT, k, vT]
    specs = [
        pl.BlockSpec((1, 1, R, KD, tq), lambda b, g, i: (b, g, 0, 0, i)),
        pl.BlockSpec((1, 1, S, KD), lambda b, g, i: (b, g, 0, 0)),
        pl.BlockSpec((1, 1, HEAD_DIM, S), lambda b, g, i: (b, g, 0, 0)),
    ]
    if bits is not None:
        ins.append(bits)
        specs.append(pl.BlockSpec((1, 1, SUBLANES, tq), lambda b, g, i: (b, g, 0, i)))
    if gates_t is not None:
        ins.append(gates_t)
        specs.append(pl.BlockSpec((1, 1, 16, tq), lambda b, g, i: (b, g, 0, i)))
    if sink_t is not None:
        ins += [sink_t, slope_t]
        specs += [pl.BlockSpec((1, R, 1, tq), lambda b, g, i: (g, 0, 0, 0))] * 2
    return pl.pallas_call(
        functools.partial(_flash_kernel, tq=tq, tk=tk, window=window, use_bits=bits is not None,
                          gate_row=gate_row if gates_t is not None else None, use_sink=sink_t is not None),
        grid=(B, G, S // tq), in_specs=specs,
        out_specs=pl.BlockSpec((1, 1, R, HEAD_DIM, tq), lambda b, g, i: (b, g, 0, 0, i)),
        out_shape=jax.ShapeDtypeStruct((B, G, R, HEAD_DIM, S), F32),
        scratch_shapes=[pltpu.VMEM((R, 1, tq), F32), pltpu.VMEM((R, 1, tq), F32),
                        pltpu.VMEM((R, HEAD_DIM, tq), F32)],
        compiler_params=_cparams(("parallel", "parallel", "parallel")), name=name,
    )(*ins)


def _softmax_parts(s, valid, extra_s, extra_valid, sink):
    s = jnp.where(valid, s, NEG)
    m = jnp.max(s, axis=-1, keepdims=True)
    if extra_s is not None:
        extra_s = jnp.where(extra_valid, extra_s, NEG)
        m = jnp.maximum(m, extra_s)
    if sink is not None:
        m = jnp.maximum(m, sink)
    e = jnp.exp(s - m)
    l = jnp.sum(e, axis=-1, keepdims=True)
    e_new = None
    if extra_s is not None:
        e_new = jnp.where(extra_valid, jnp.exp(extra_s - m), 0.0)
        l = l + e_new
    if sink is not None:
        l = l + jnp.exp(sink - m)
    return e, e_new, l


def _new_key_score(q16, k_new):
    return jnp.sum(q16.astype(F32) * k_new.astype(BF16).astype(F32), axis=-1, keepdims=True)


def _dec_cmp_kernel(q_ref, kT_ref, vT_ref, slope_ref, gate_ref, ov_ref, o_ref, idx_ref, *, qpos, n_past_blocks):
    q16 = q_ref[0]
    slope = slope_ref[...]
    nb = kT_ref.shape[-1]
    nsp = ov_ref.shape[1]
    blk_end = lax.broadcasted_iota(jnp.int32, (1, nb), 1) * CMP_STRIDE + (CMP_LEN - 1)
    valid = blk_end <= qpos
    bias = slope * (blk_end - qpos).astype(F32)
    row = lax.broadcasted_iota(jnp.int32, (H_A, 1), 0)
    lane = lax.broadcasted_iota(jnp.int32, (1, nsp), 1)
    cur = qpos // SEL_BLOCK
    forced = (lane == 0) | (lane == cur) | (lane == cur - 1)
    ns = n_past_blocks + 1
    o = jnp.zeros((H_A, HEAD_DIM), F32)
    tile_l = lax.broadcasted_iota(jnp.int32, (SUBLANES, LANES), 1)
    row8 = lax.broadcasted_iota(jnp.int32, (SUBLANES, 1), 0)
    score8 = jnp.full((SUBLANES, nsp), _LOWEST, F32)
    gs = range(G_A)
    ss = [jnp.where(valid, jnp.dot(q16, kT_ref[0, 0, g], preferred_element_type=F32) + bias, NEG) for g in gs]
    ms = [jnp.max(ss[g], axis=-1, keepdims=True) for g in gs]
    es = [jnp.exp(ss[g] - ms[g]) for g in gs]
    ls = [jnp.sum(es[g], axis=-1, keepdims=True) for g in gs]
    pbs = [(es[g] / ls[g]).astype(BF16) for g in gs]
    ogs = [lax.dot_general(pbs[g], vT_ref[0, 0, g], (((1,), (1,)), ((), ())), preferred_element_type=F32)
           for g in gs]
    imps = [jnp.dot(pbs[g], ov_ref[...], preferred_element_type=F32) for g in gs]
    for g in gs:
        mine = (row >> 2) == g
        o = jnp.where(mine, ogs[g], o)
        imp = jnp.sum(jnp.where(mine, imps[g], 0.0), axis=0, keepdims=True)
        score = jnp.where(lane * SEL_BLOCK <= qpos, imp + jnp.where(forced, FORCE, 0.0), NEG)
        score = jnp.where(lane < ns, score, _LOWEST)
        score8 = jnp.where(row8 == g, score, score8)
    tile = jnp.zeros((SUBLANES, LANES), jnp.int32)
    for r, idx in enumerate(_select_rounds(score8, lane)):
        tile = jnp.where(tile_l == r, idx, tile)
    gate = 1.0 / (1.0 + jnp.exp(-gate_ref[0][:, 0:1]))
    o_ref[0] = o * gate
    idx_ref[0] = tile


def _dec_cmp(q16, kvcT, slope16, gates16, ov, *, qpos, n_past_blocks):
    Bd = q16.shape[0]
    nb = kvcT.shape[-1]
    return pl.pallas_call(
        functools.partial(_dec_cmp_kernel, qpos=qpos, n_past_blocks=n_past_blocks),
        grid=(Bd,),
        in_specs=[
            pl.BlockSpec((1, H_A, HEAD_DIM), lambda b: (b, 0, 0)),
            pl.BlockSpec((1, 1, G_A, HEAD_DIM, nb), lambda b: (b, 0, 0, 0, 0)),
            pl.BlockSpec((1, 1, G_A, HEAD_DIM, nb), lambda b: (b, 1, 0, 0, 0)),
            pl.BlockSpec((H_A, 1), lambda b: (0, 0)),
            pl.BlockSpec((1, H_A, 4), lambda b: (b, 0, 0)),
            pl.BlockSpec(ov.shape, lambda b: (0, 0)),
        ],
        out_specs=[pl.BlockSpec((1, H_A, HEAD_DIM), lambda b: (b, 0, 0)),
                   pl.BlockSpec((1, SUBLANES, LANES), lambda b: (b, 0, 0))],
        out_shape=[jax.ShapeDtypeStruct((Bd, H_A, HEAD_DIM), F32),
                   jax.ShapeDtypeStruct((Bd, SUBLANES, LANES), jnp.int32)],
        compiler_params=_cparams(("parallel",)), name="dec_cmp",
    )(q16, kvcT, kvcT, slope16, gates16, ov)


def _dec_sel_kernel(idx_ref, pt_ref, *refs, qpos, n_past_blocks):
    k_refs = refs[:N_SEL]
    v_refs = refs[N_SEL:2 * N_SEL]
    q_ref, kn_ref, vn_ref, slope_ref, gate_ref, o_ref = refs[2 * N_SEL:]
    b = pl.program_id(0)
    g = pl.program_id(1)
    q16 = q_ref[0]
    slope = slope_ref[...]
    lane = lax.broadcasted_iota(jnp.int32, (1, PAGE_SIZE), 1)
    s_l, valid_l, v_l = [], [], []
    any_new = None
    for kk in range(N_SEL):
        idx = idx_ref[(b * G_A + g) * N_SEL + kk]
        is_new = (jnp.zeros((H_A, 1), jnp.int32) + idx) >= n_past_blocks
        any_new = is_new if any_new is None else (any_new | is_new)
        jp = jnp.minimum(idx, n_past_blocks - 1)
        kpos = (jp >> 1) * PAGE_SIZE + lane
        s = jnp.dot(q16, k_refs[kk][0, 0, 0].astype(BF16), preferred_element_type=F32)
        s_l.append(s - slope * (qpos - kpos).astype(F32))
        in_block = ((lane >> 6) == (jp & 1)) & ((lane * 0 + idx) < n_past_blocks) & (kpos <= qpos)
        valid_l.append(jnp.broadcast_to(in_block, (H_A, PAGE_SIZE)))
        v_l.append(v_refs[kk][0, 0, 0].astype(BF16))
    s = jnp.concatenate(s_l, axis=-1)
    valid = jnp.concatenate(valid_l, axis=-1)
    vT = jnp.concatenate(v_l, axis=-1)
    s_new = _new_key_score(q16, kn_ref[0, 0, 0])
    e, e_new, l = _softmax_parts(s, valid, s_new, any_new, None)
    o = lax.dot_general(e.astype(BF16), vT, (((1,), (1,)), ((), ())), preferred_element_type=F32)
    o = o + e_new * vn_ref[0, 0, 0].astype(BF16).astype(F32)
    gate = 1.0 / (1.0 + jnp.exp(-gate_ref[0][:, 1:2]))
    o_ref[0, 0] = (o / l) * gate


def _dec_sel(idx_flat, page_table, cacheT, q16, kv_new, slope16, gates16, *, qpos, n_past_blocks):
    Bd = q16.shape[0]

    def tile_spec(kk, kind):
        def imap(b, g, idx, pt):
            jp = jnp.minimum(idx[(b * G_A + g) * N_SEL + kk], n_past_blocks - 1)
            return (pt[b, jp >> 1], kind, g, 0, 0)
        return pl.BlockSpec((1, 1, 1, HEAD_DIM, PAGE_SIZE), imap)

    return pl.pallas_call(
        functools.partial(_dec_sel_kernel, qpos=qpos, n_past_blocks=n_past_blocks),
        grid_spec=pltpu.PrefetchScalarGridSpec(
            num_scalar_prefetch=2, grid=(Bd, G_A),
            in_specs=[tile_spec(kk, 0) for kk in range(N_SEL)] + [tile_spec(kk, 1) for kk in range(N_SEL)] + [
                pl.BlockSpec((1, H_A, HEAD_DIM), lambda b, g, idx, pt: (b, 0, 0)),
                pl.BlockSpec((1, 1, 1, 1, HEAD_DIM), lambda b, g, idx, pt: (b, 0, g, 0, 0)),
                pl.BlockSpec((1, 1, 1, 1, HEAD_DIM), lambda b, g, idx, pt: (b, 1, g, 0, 0)),
                pl.BlockSpec((H_A, 1), lambda b, g, idx, pt: (0, 0)),
                pl.BlockSpec((1, H_A, 4), lambda b, g, idx, pt: (b, 0, 0)),
            ],
            out_specs=pl.BlockSpec((1, 1, H_A, HEAD_DIM), lambda b, g, idx, pt: (b, g, 0, 0)),
        ),
        out_shape=jax.ShapeDtypeStruct((Bd, G_A, H_A, HEAD_DIM), F32),
        compiler_params=_cparams(("parallel", "parallel")), name="dec_sel",
    )(idx_flat, page_table, *([cacheT] * (2 * N_SEL)), q16, kv_new, kv_new, slope16, gates16)


def _dec_slab_kernel(*refs, n_groups, window, qpos, gate_col, use_sink):
    it = iter(refs)
    q_ref, kT_ref, vT_ref, kn_ref, vn_ref, slope_ref = (next(it) for _ in range(6))
    gate_ref = next(it) if gate_col is not None else None
    sink_ref = next(it) if use_sink else None
    o_ref = next(it)
    q16 = q_ref[0]
    slope = slope_ref[...]
    nbuf = kT_ref.shape[-1]
    rpg = q16.shape[0] // n_groups
    kpos = qpos - nbuf + lax.broadcasted_iota(jnp.int32, (1, nbuf), 1)
    dist = qpos - kpos
    valid = (dist >= 0) & (dist < window) & (kpos >= 0)
    bias = -slope * dist.astype(F32)
    row = lax.broadcasted_iota(jnp.int32, (q16.shape[0], 1), 0)
    sink = sink_ref[...] if use_sink else None
    o = jnp.zeros((q16.shape[0], HEAD_DIM), F32)
    for g in range(n_groups):
        s = jnp.dot(q16, kT_ref[0, 0, g].astype(BF16), preferred_element_type=F32) + bias
        s_new = _new_key_score(q16, kn_ref[0, 0, g:g + 1, :])
        e, e_new, l = _softmax_parts(s, valid, s_new, True, sink)
        og = lax.dot_general(e.astype(BF16), vT_ref[0, 0, g].astype(BF16), (((1,), (1,)), ((), ())),
                             preferred_element_type=F32)
        og = (og + e_new * vn_ref[0, 0, g:g + 1, :].astype(BF16).astype(F32)) / l
        o = jnp.where((row // rpg) == g, og, o)
    if gate_col is not None:
        o = o * (1.0 / (1.0 + jnp.exp(-gate_ref[0][:, gate_col:gate_col + 1])))
    o_ref[0] = o


def _dec_slab(q16, slabT, kv_new, slope16, *, window, qpos, gates16=None, gate_col=None, sink16=None, name):
    Bd, _, G, _, nbuf = slabT.shape
    H = q16.shape[1]
    ins = [q16, slabT, slabT, kv_new, kv_new, slope16]
    specs = [
        pl.BlockSpec((1, H, HEAD_DIM), lambda b: (b, 0, 0)),
        pl.BlockSpec((1, 1, G, HEAD_DIM, nbuf), lambda b: (b, 0, 0, 0, 0)),
        pl.BlockSpec((1, 1, G, HEAD_DIM, nbuf), lambda b: (b, 1, 0, 0, 0)),
        pl.BlockSpec((1, 1, G, HEAD_DIM), lambda b: (b, 0, 0, 0)),
        pl.BlockSpec((1, 1, G, HEAD_DIM), lambda b: (b, 1, 0, 0)),
        pl.BlockSpec((H, 1), lambda b: (0, 0)),
    ]
    if gates16 is not None:
        ins.append(gates16)
        specs.append(pl.BlockSpec((1, H, 4), lambda b: (b, 0, 0)))
    if sink16 is not None:
        ins.append(sink16)
        specs.append(pl.BlockSpec((H, 1), lambda b: (0, 0)))
    return pl.pallas_call(
        functools.partial(_dec_slab_kernel, n_groups=G, window=window, qpos=qpos,
                          gate_col=gate_col if gates16 is not None else None, use_sink=sink16 is not None),
        grid=(Bd,), in_specs=specs,
        out_specs=pl.BlockSpec((1, H, HEAD_DIM), lambda b: (b, 0, 0)),
        out_shape=jax.ShapeDtypeStruct((Bd, H, HEAD_DIM), F32),
        compiler_params=_cparams(("parallel",)), name=name,
    )(*ins)


def _alibi(n_heads):
    return 2.0 ** (-8.0 * jnp.arange(1, n_heads + 1, dtype=F32) / n_heads)


def _heads_first(z, G, R):
    B, S, _ = z.shape
    return z.reshape(B, S, G, R, HEAD_DIM).transpose(0, 2, 3, 1, 4)


def _heads_last(o):
    B, G, R, S, _ = o.shape
    return o.transpose(0, 3, 1, 2, 4).reshape(B, S, G * R * HEAD_DIM)


def kernel(x_prompt, x_sample, c_prompt, c_sample, cache_cmp_kv, cache_sel_kv, state_win_kv, state_shared_kv, page_table, w_ada, b_ada, g_norm, w_in_a, w_o_a, cmp_pos, w_cmp1, b_cmp1, w_cmp2, g_kv, w_ada_kv, b_ada_kv, w_kv_b, w_q_b, w_o_b, sinks, w_pq, peer_keys, peer_u, peer_v, g_final):
    B, S, D = x_prompt.shape
    Bd = x_sample.shape[0]
    n_pages = page_table.shape[1]
    past_len = n_pages * PAGE_SIZE
    n_past_blocks = past_len // SEL_BLOCK
    TQ = 256
    TQ_CMP = 512
    bf = lambda a: a.astype(BF16)

    c_all = jnp.concatenate([c_prompt, c_sample], axis=0)[None]
    nc = B + Bd

    def ada(w, b):
        return _mm([c_all], bf(w), tm=nc, tn=2048, prologue="silu", bias=b[None], name="ada")[0]

    mods = [ada(w_ada[i], b_ada[i]) for i in range(2)]
    mods_kv = ada(w_ada_kv, b_ada_kv)

    def split_mods(m, n):
        parts = jnp.split(m, n, axis=-1)
        return [p[:B, None, :] for p in parts], [p[None, B:, :] for p in parts]

    mp0, ms0 = split_mods(mods[0], 6)
    mp1, ms1 = split_mods(mods[1], 6)
    mpk, msk = split_mods(mods_kv, 2)

    w_in = bf(jnp.pad(w_in_a[0], ((0, 0), (0, 21 * LANES - NSA_IN))))
    w1p, w2T = _cmp_weights(w_cmp1[0], w_cmp2[0])
    pos_rows = jnp.pad(cmp_pos[0].reshape(2, 1, CMP_LEN * HEAD_DIM), ((0, 0), (0, SUBLANES - 1), (0, 0)))
    b1 = jnp.stack([_mm([pos_rows[k:k + 1]], bf(w_cmp1[0, k]), tm=SUBLANES, bias=b_cmp1[0, k][None],
                        name="cmp_bias")[0, 0:1] for k in range(2)])
    slopes_a = _alibi(H_A)
    slopes_b = _alibi(H_B)
    gn = lambda v: v.reshape(1, D)
    peer_w = [(bf(w_pq[i].T), bf(peer_keys[i].reshape(2 * PEER_HEADS, N_KEYS, N_KEYS)), bf(peer_u[i]), bf(peer_v[i]))
              for i in range(2)]

    def rows_of(v, G, R, tq=TQ):
        return jnp.repeat(v.reshape(G, R), tq, axis=1)[..., None]

    z = _mm([x_prompt], w_in, tm=512, norm=(gn(g_norm[0, 0]), mp0[1], mp0[0]), name="nsa_in_p")
    kv_c, kv_s, kv_w = (z[..., NSA_Q + i * NSA_KV:NSA_Q + (i + 1) * NSA_KV] for i in range(3))
    q_t = bf(_heads_first(z[..., :NSA_Q] * (HEAD_DIM ** -0.5), G_A, R_A))
    glog = z[..., NSA_Q + 3 * NSA_KV:NSA_IN].reshape(B, S, 3, G_A, R_A).transpose(0, 3, 1, 2, 4)
    gates_t = jnp.pad(glog.reshape(B, G_A, S, 3 * R_A), ((0, 0), (0, 0), (0, 0), (0, 16 - 3 * R_A)))

    kvcT = _cmp_prompt(kv_c, w1p, b1, w2T)
    nb_p = S // CMP_STRIDE - 1
    ns_p = S // SEL_BLOCK
    ov_p = _overlap_matrix(kvcT.shape[-1], LANES, nb_p, ns_p)
    o_c, bits = _cmp_attend_prompt(q_t, kvcT, rows_of(slopes_a, G_A, R_A, TQ_CMP), gates_t, ov_p,
                                   tq=TQ_CMP, nb=nb_p, ns=ns_p)

    kpos = jnp.arange(S)
    kp_hi, kp_lo = ((kpos >> 6) * SEL_BLOCK).astype(F32), (kpos & (SEL_BLOCK - 1)).astype(F32)
    feat_k = bf(jnp.pad(jnp.stack([kp_hi, kp_lo, kp_hi, kp_lo], axis=-1), ((0, 0), (0, _QK_FEAT - 4))))

    def split_slopes(slopes, G, R):
        hi = bf(slopes)
        lo = bf(slopes - hi.astype(F32))
        rows = jnp.pad(jnp.stack([hi, hi, lo, lo], axis=-1), ((0, 0), (0, _QK_FEAT - 4)))
        eff = hi.astype(F32) + lo.astype(F32)
        return rows.reshape(G, R, _QK_FEAT), jnp.broadcast_to(eff.reshape(G, R, 1, 1), (G, R, 1, TQ))

    def flash_q(qz, feat_q, G, R):
        qT = bf(qz.reshape(B, S, G, R, HEAD_DIM) * (HEAD_DIM ** -0.5)).transpose(0, 2, 3, 4, 1)
        return jnp.concatenate([qT, jnp.broadcast_to(feat_q[None, :, :, :, None], (B, G, R, _QK_FEAT, S))], axis=3)

    def flash_kv(kv, G):
        kv5 = kv.reshape(B, S, 2, G, HEAD_DIM)
        k = jnp.concatenate([bf(kv5[:, :, 0].transpose(0, 2, 1, 3)),
                             jnp.broadcast_to(feat_k[None, None], (B, G, S, _QK_FEAT))], axis=-1)
        return k, bf(kv5[:, :, 1].transpose(0, 2, 3, 1))

    def tokens_first(oT):
        return oT.transpose(0, 4, 1, 2, 3).reshape(B, S, -1)

    feat_qa, _ = split_slopes(slopes_a, G_A, R_A)
    qT_a = flash_q(z[..., :NSA_Q], feat_qa, G_A, R_A)
    gates_tt = jnp.pad(glog.transpose(0, 1, 3, 4, 2).reshape(B, G_A, 3 * R_A, S), ((0, 0), (0, 0), (0, 16 - 3 * R_A), (0, 0)))
    k_s, vT_s = flash_kv(kv_s, G_A)
    o_s = _flash(qT_a, k_s, vT_s, tq=TQ, tk=256, bits=bits, gates_t=gates_tt, gate_row=1, name="sel_p")
    k_w, vT_w = flash_kv(kv_w, G_A)
    o_w = _flash(qT_a, k_w, vT_w, tq=TQ, tk=256, window=WIN_A, gates_t=gates_tt, gate_row=2, name="win_p")
    x1 = _mm([_heads_last(o_c), tokens_first(o_s), tokens_first(o_w)], bf(w_o_a[0]), tm=512,
             res=(x_prompt, mp0[2]), name="nsa_out_p")
    x1 = _peer(x1, mp0[4], mp0[3], mp0[5], gn(g_norm[0, 1]), *peer_w[0], t_tile=512, ei=8)

    kv_sh = _mm([x1], bf(w_kv_b), tm=512, norm=(gn(g_kv), mpk[1], mpk[0]), name="kv_sh_p")
    q1 = _mm([x1], bf(w_q_b[0]), tm=512, norm=(gn(g_norm[1, 0]), mp1[1], mp1[0]), name="q_b_p")
    feat_qb, slope_tb = split_slopes(slopes_b, G_B, R_B)
    k_b, vT_b = flash_kv(kv_sh, G_B)
    sink_tb = jnp.broadcast_to(sinks[0].reshape(G_B, R_B, 1, 1), (G_B, R_B, 1, TQ))
    o_b = _flash(flash_q(q1, feat_qb, G_B, R_B), k_b, vT_b, tq=TQ, tk=256, window=WIN_B,
                 sink_t=sink_tb, slope_t=slope_tb, name="swa_p")
    x2 = _mm([tokens_first(o_b)], bf(w_o_b[0]), tm=512, res=(x1, mp1[2]), name="swa_out_p")
    x2 = _peer(x2, mp1[4], mp1[3], mp1[5], gn(g_norm[1, 1]), *peer_w[1], t_tile=512, ei=8)
    y_prompt = _rmsnorm(x2, gn(g_final), tm=512)

    xs = x_sample.reshape(1, Bd, D)
    zs = _mm([xs], w_in, tm=Bd, norm=(gn(g_norm[0, 0]), ms0[1], ms0[0]), name="nsa_in_s")[0]
    kvs_new = [zs[:, NSA_Q + i * NSA_KV:NSA_Q + (i + 1) * NSA_KV] for i in range(3)]
    q16 = bf(zs[:, :NSA_Q] * (HEAD_DIM ** -0.5)).reshape(Bd, H_A, HEAD_DIM)
    gates16 = jnp.pad(zs[:, NSA_Q + 3 * NSA_KV:NSA_IN].reshape(Bd, 3, H_A).transpose(0, 2, 1), ((0, 0), (0, 0), (0, 1)))
    slope16_a = slopes_a.reshape(H_A, 1)

    cmpT = cache_cmp_kv[0].transpose(0, 2, 3, 4, 1)
    n_phys = cmpT.shape[0]
    kvc_new = jnp.pad(kvs_new[0][:, None, :], ((0, 0), (0, SUBLANES - 1), (0, 0)))
    kvcT_s = _cmp_sample(cmpT.reshape(n_phys, 2, 2, 2 * HEAD_DIM, PAGE_SIZE), page_table, kvc_new, w1p, b1, w2T)
    nb_s = kvcT_s.shape[-1] - 1
    ns_pad = 2 * LANES
    ov_s = _overlap_matrix(kvcT_s.shape[-1], ns_pad, nb_s + 1, n_past_blocks + 1)
    o_cs, idx_tile = _dec_cmp(q16, kvcT_s, slope16_a, gates16, ov_s, qpos=past_len, n_past_blocks=n_past_blocks)
    idx_flat = idx_tile[:, :G_A, :N_SEL].reshape(-1)

    selT = cache_sel_kv[0].transpose(0, 2, 3, 4, 1)
    kvs_new5 = kvs_new[1].reshape(Bd, 2, G_A, 1, HEAD_DIM)
    o_ss4 = _dec_sel(idx_flat, page_table, selT, q16, kvs_new5, slope16_a, gates16,
                     qpos=past_len, n_past_blocks=n_past_blocks)
    own = (jnp.arange(H_A) // R_A)[None, :] == jnp.arange(G_A)[:, None]
    o_ss = jnp.sum(jnp.where(own[None, :, :, None], o_ss4, 0.0), axis=1)

    winT = state_win_kv[0].transpose(0, 2, 3, 4, 1)
    o_ws = _dec_slab(q16, winT, kvs_new[2].reshape(Bd, 2, G_A, HEAD_DIM), slope16_a, window=WIN_A, qpos=past_len,
                     gates16=gates16, gate_col=2, name="dec_win")
    flat = lambda o: o.reshape(1, Bd, H_A * HEAD_DIM)
    xs1 = _mm([flat(o_cs), flat(o_ss), flat(o_ws)], bf(w_o_a[0]), tm=Bd, res=(xs, ms0[2]), name="nsa_out_s")
    xs1 = _peer(xs1, ms0[4], ms0[3], ms0[5], gn(g_norm[0, 1]), *peer_w[0], t_tile=Bd, ei=8)

    kv_sh_s = _mm([xs1], bf(w_kv_b), tm=Bd, norm=(gn(g_kv), msk[1], msk[0]), name="kv_sh_s")[0]
    q1s = _mm([xs1], bf(w_q_b[0]), tm=Bd, norm=(gn(g_norm[1, 0]), ms1[1], ms1[0]), name="q_b_s")[0]
    q16b = bf(q1s * (HEAD_DIM ** -0.5)).reshape(Bd, H_B, HEAD_DIM)
    shT = state_shared_kv.transpose(0, 2, 3, 4, 1)
    o_bs = _dec_slab(q16b, shT, kv_sh_s.reshape(Bd, 2, G_B, HEAD_DIM), slopes_b.reshape(H_B, 1), window=WIN_B,
                     qpos=past_len, sink16=sinks[0].reshape(H_B, 1), name="dec_swa")
    xs2 = _mm([flat(o_bs)], bf(w_o_b[0]), tm=Bd, res=(xs1, ms1[2]), name="swa_out_s")
    xs2 = _peer(xs2, ms1[4], ms1[3], ms1[5], gn(g_norm[1, 1]), *peer_w[1], t_tile=Bd, ei=8)
    y_sample = _rmsnorm(xs2, gn(g_final), tm=Bd).reshape(Bd, 1, D)

    as_kv = lambda a, n, G: a.reshape(n, -1, 2, G, HEAD_DIM)
    nw = min(WIN_A, S)
    nwb = min(WIN_B, S)
    new_cmp_prompt = as_kv(kv_c, B, G_A)[None]
    new_sel_prompt = as_kv(kv_s, B, G_A)[None]
    new_win_prompt = as_kv(kv_w, B, G_A)[None, :, S - nw:]
    new_cmp_sample = as_kv(kvs_new[0], Bd, G_A)[None]
    new_sel_sample = as_kv(kvs_new[1], Bd, G_A)[None]
    new_win_sample = jnp.concatenate([state_win_kv[0], as_kv(kvs_new[2], Bd, G_A)], axis=1)[None, :, 1:]
    new_shared_prompt = as_kv(kv_sh, B, G_B)[:, S - nwb:]
    new_shared_sample = jnp.concatenate([state_shared_kv, as_kv(kv_sh_s, Bd, G_B)], axis=1)[:, 1:]
    return (y_prompt, y_sample, new_cmp_prompt, new_cmp_sample, new_sel_prompt, new_sel_sample,
            new_win_prompt, new_win_sample, new_shared_prompt, new_shared_sample)
```

```python
import functools

import jax
import jax.numpy as jnp
from jax import lax
from jax.experimental import pallas as pl
from jax.experimental.pallas import tpu as pltpu

F32 = jnp.float32
BF16 = jnp.bfloat16

D_MODEL = 1024
HEAD_DIM = 64
PAGE_SIZE = 128
H_A = 16
G_A = 4
R_A = H_A // G_A
CMP_LEN = 32
CMP_STRIDE = 16
CMP_HID = 128
SEL_BLOCK = 64
N_SEL = 8
WIN_A = 512
NSA_Q = H_A * HEAD_DIM
NSA_KV = 2 * G_A * HEAD_DIM
NSA_IN = NSA_Q + 3 * NSA_KV + 3 * H_A
H_B = 16
G_B = 2
R_B = H_B // G_B
WIN_B = 128
PEER_HEADS = 8
PEER_QDIM = 256
N_KEYS = 128
PEER_TOPK = 16
RMS_EPS = 1e-6
NEG = -1e30
FORCE = 1e3

LANES = 128
SUBLANES = 8
VMEM_LIMIT = 56 * 1024 * 1024


def _cparams(sem):
    return pltpu.CompilerParams(dimension_semantics=sem, vmem_limit_bytes=VMEM_LIMIT)


_GELU_C0 = 0.7978845608028654
_GELU_C1 = 0.7978845608028654 * 0.044715


def _gelu(x):
    return x * (0.5 * (1.0 + jnp.tanh(0.7978845608028654 * (x + 0.044715 * (x * x * x)))))


def _normmod(x, g, sc, sh):
    r = lax.rsqrt(jnp.mean(x * x, axis=-1, keepdims=True) + RMS_EPS)
    return (x * r) * g * (1.0 + sc) + sh


N_TOPV = PEER_TOPK + 1
_PEER_JB = 32
_PAIRS = [(p, q) for p in range(N_TOPV) for q in range(N_TOPV) if (p + 1) * (q + 1) <= N_TOPV]


def _peer_prep_kernel(x_ref, sc_ref, sh_ref, g_ref, wpqT_ref, keys_ref,
                      hT_ref, s1_ref, b_ref, c_ref, a_ref, qT_ref, topv_ref):
    T = x_ref.shape[1]
    h = _normmod(x_ref[0], g_ref[...], sc_ref[0], sh_ref[0])
    hT = h.T.astype(BF16)
    hT_ref[...] = hT
    qT_ref[...] = jnp.dot(wpqT_ref[...], hT, preferred_element_type=F32).astype(BF16)

    halves = (c_ref, s1_ref)
    for hc in range(2 * PEER_HEADS):
        q = qT_ref[hc * N_KEYS:(hc + 1) * N_KEYS, :]
        halves[hc % 2][hc // 2] = jnp.dot(keys_ref[hc], q, preferred_element_type=F32)

    topv_ref[:, 0] = jnp.full((2, PEER_HEADS, T), jnp.inf, F32)

    def topk_round(r, carry):
        for hc in range(2 * PEER_HEADS):
            c, hh = hc % 2, hc // 2
            s = halves[c][hh]
            below = jnp.where(s < topv_ref[c, r, hh:hh + 1, :], s, -jnp.inf)
            topv_ref[c, r + 1, hh:hh + 1, :] = jnp.max(below, axis=0, keepdims=True)
        return carry

    lax.fori_loop(0, N_TOPV, topk_round, 0)

    for tc in range(T // LANES):
        sl = slice(tc * LANES, (tc + 1) * LANES)
        av = [topv_ref[0, p + 1, :, sl] for p in range(N_TOPV)]
        bv = [topv_ref[1, q + 1, :, sl] for q in range(N_TOPV)]
        cands = [av[p] + bv[q] for (p, q) in _PAIRS]
        top = av[0] + bv[0]
        work = list(cands)
        kth = None
        prev = None
        for r in range(N_TOPV):
            m = work[0]
            for w in work[1:]:
                m = jnp.maximum(m, w)
            prev, kth = kth, m
            work = [jnp.where(w >= m, -jnp.inf, w) for w in work]
        tau = 0.5 * (prev + kth)
        z = jnp.zeros_like(top)
        for cnd in cands:
            z = z + jnp.where(cnd >= tau, jnp.exp(cnd - top), 0.0)
        topv_ref[0, 0, :, sl] = tau
        topv_ref[1, 0, :, sl] = 1.0 / z

    for hh in range(PEER_HEADS):
        a1 = topv_ref[0, 1, hh:hh + 1, :]
        b1 = topv_ref[1, 1, hh:hh + 1, :]
        tau = topv_ref[0, 0, hh:hh + 1, :]
        rz = topv_ref[1, 0, hh:hh + 1, :]
        s0 = c_ref[hh]
        c_ref[hh] = tau - s0
        a_ref[hh] = jnp.exp(s0 - a1) * (0.5 * rz)
        b_ref[hh] = jnp.exp(s1_ref[hh] - b1)


def _peer_dense_kernel(hT_ref, s1_ref, b_ref, c_ref, a_ref, u_ref, v_ref, x_ref, ga_ref,
                       o_ref, acc_ref, sa_ref, sb_ref, wa_ref, wb_ref, *, n_i, mw):
    i = pl.program_id(2)
    T = hT_ref.shape[1]
    ei = u_ref.shape[0] // N_KEYS
    s_bufs = (sa_ref, sb_ref)
    w_bufs = (wa_ref, wb_ref)

    tw = min(T, 2 * LANES)
    n_tw = T // tw

    blk = ei * N_KEYS

    def stage_a(dst):
        def piece(k, m):
            s_bufs[dst][m * mw:(m + 1) * mw, k * tw:(k + 1) * tw] = jnp.dot(
                u_ref[m * mw:(m + 1) * mw, :], hT_ref[:, k * tw:(k + 1) * tw], preferred_element_type=F32)
        return [functools.partial(piece, k, m) for k in range(n_tw) for m in range(blk // mw)]

    def stage_b(src):
        s_ref, w_ref = s_bufs[src], w_bufs[src]

        def block(ii, jb):
            jr = slice(jb * _PEER_JB, (jb + 1) * _PEER_JB)
            rows = slice(ii * N_KEYS + jb * _PEER_JB, ii * N_KEYS + (jb + 1) * _PEER_JB)
            w = jnp.zeros((_PEER_JB, T), F32)
            for hh in range(PEER_HEADS):
                cth = c_ref[hh, ii:ii + 1, :]
                ath = a_ref[hh, ii:ii + 1, :]
                w = w + jnp.where(s1_ref[hh, jr, :] >= cth, b_ref[hh, jr, :], 0.0) * ath
            s = s_ref[rows, :]
            g2 = s * (1.0 + jnp.tanh(s * (_GELU_C0 + _GELU_C1 * (s * s))))
            w_ref[rows, :] = (w * g2).astype(BF16)
        return [functools.partial(block, ii, jb) for ii in range(ei) for jb in range(N_KEYS // _PEER_JB)]

    def stage_c(src):
        cw = min(mw, acc_ref.shape[1])

        def piece(k, m):
            acc_ref[k * tw:(k + 1) * tw, m * cw:(m + 1) * cw] += lax.dot_general(
                w_bufs[src][:, k * tw:(k + 1) * tw], v_ref[:, m * cw:(m + 1) * cw], (((0,), (0,)), ((), ())),
                preferred_element_type=F32)
        return [functools.partial(piece, k, m) for k in range(n_tw) for m in range(acc_ref.shape[1] // cw)]

    def run_interleaved(mxu_pieces, vpu_blocks):
        n = len(mxu_pieces)
        per = -(-len(vpu_blocks) // n)
        for k, piece in enumerate(mxu_pieces):
            piece()
            for blk_fn in vpu_blocks[k * per:(k + 1) * per]:
                blk_fn()

    @pl.when(i == 0)
    def _():
        acc_ref[...] = jnp.zeros_like(acc_ref)
        wa_ref[...] = jnp.zeros_like(wa_ref)
        wb_ref[...] = jnp.zeros_like(wb_ref)
        run_interleaved(stage_a(0), [])

    steady = (i >= 1) & (i <= n_i)
    for par in range(2):
        @pl.when(steady & (i % 2 == par))
        def _(par=par):
            run_interleaved(stage_a(par) + stage_c(par), stage_b(1 - par))

    @pl.when(i == n_i + 1)
    def _():
        run_interleaved(stage_c((n_i + 1) % 2), [])
        o_ref[0] = x_ref[0] + ga_ref[0] * acc_ref[...]


def _mod_spec(per_row, t):
    if per_row:
        return pl.BlockSpec((1, t, D_MODEL), lambda b, j, *_: (0, j, 0))
    return pl.BlockSpec((1, 1, D_MODEL), lambda b, j, *_: (b, 0, 0))


def _peer(x, sc, sh, ga, g, wpqT, keys16, u_bf, v_bf, *, t_tile, ei, mw=8 * LANES):
    B, S, D = x.shape
    per_row = sc.shape[1] != 1
    nt = S // t_tile
    ntot = B * S
    n_i = N_KEYS // ei
    col = lambda b, j, *_: (0, b * nt + j)
    col3 = lambda b, j, *_: (0, 0, b * nt + j)
    row3 = lambda b, j, *_: (b, j, 0)
    f = jax.ShapeDtypeStruct
    hT, s1, bb, cc, aa = pl.pallas_call(
        _peer_prep_kernel,
        grid=(B, nt),
        in_specs=[
            pl.BlockSpec((1, t_tile, D), row3),
            _mod_spec(per_row, t_tile), _mod_spec(per_row, t_tile),
            pl.BlockSpec((1, D), lambda b, j: (0, 0)),
            pl.BlockSpec((PEER_HEADS * PEER_QDIM, D), lambda b, j: (0, 0)),
            pl.BlockSpec((2 * PEER_HEADS, N_KEYS, N_KEYS), lambda b, j: (0, 0, 0)),
        ],
        out_specs=[
            pl.BlockSpec((D, t_tile), col),
            pl.BlockSpec((PEER_HEADS, N_KEYS, t_tile), col3),
            pl.BlockSpec((PEER_HEADS, N_KEYS, t_tile), col3),
            pl.BlockSpec((PEER_HEADS, N_KEYS, t_tile), col3),
            pl.BlockSpec((PEER_HEADS, N_KEYS, t_tile), col3),
        ],
        out_shape=[f((D, ntot), BF16)] + [f((PEER_HEADS, N_KEYS, ntot), F32)] * 4,
        scratch_shapes=[pltpu.VMEM((PEER_HEADS * PEER_QDIM, t_tile), BF16),
                        pltpu.VMEM((2, N_TOPV + 1, PEER_HEADS, t_tile), F32)],
        compiler_params=_cparams(("parallel", "parallel")),
        name="peer_prep",
    )(x, sc, sh, g, wpqT, keys16)

    blk = ei * N_KEYS
    blk_a = lambda i: jnp.minimum(i, n_i - 1)
    blk_b = lambda i: jnp.clip(i - 1, 0, n_i - 1)
    blk_c = lambda i: jnp.clip(i - 2, 0, n_i - 1)
    return pl.pallas_call(
        functools.partial(_peer_dense_kernel, n_i=n_i, mw=mw),
        grid=(B, nt, n_i + 2),
        in_specs=[
            pl.BlockSpec((D, t_tile), col),
            pl.BlockSpec((PEER_HEADS, N_KEYS, t_tile), col3),
            pl.BlockSpec((PEER_HEADS, N_KEYS, t_tile), col3),
            pl.BlockSpec((PEER_HEADS, ei, t_tile), lambda b, j, i: (0, blk_b(i), b * nt + j)),
            pl.BlockSpec((PEER_HEADS, ei, t_tile), lambda b, j, i: (0, blk_b(i), b * nt + j)),
            pl.BlockSpec((blk, D), lambda b, j, i: (blk_a(i), 0)),
            pl.BlockSpec((blk, D), lambda b, j, i: (blk_c(i), 0)),
            pl.BlockSpec((1, t_tile, D), row3),
            _mod_spec(per_row, t_tile),
        ],
        out_specs=pl.BlockSpec((1, t_tile, D), row3),
        out_shape=f((B, S, D), F32),
        scratch_shapes=[pltpu.VMEM((t_tile, D), F32),
                        pltpu.VMEM((blk, t_tile), F32), pltpu.VMEM((blk, t_tile), F32),
                        pltpu.VMEM((blk, t_tile), BF16), pltpu.VMEM((blk, t_tile), BF16)],
        compiler_params=_cparams(("parallel", "parallel", "arbitrary")),
        name="peer_dense",
    )(hT, s1, bb, cc, aa, u_bf, v_bf, x, ga)


def _mm_kernel(*refs, n_in, prologue, has_bias, has_res):
    it = iter(refs)
    a_refs = [next(it) for _ in range(n_in)]
    if prologue == "normmod":
        g_ref, sc_ref, sh_ref = next(it), next(it), next(it)
    w_ref = next(it)
    b_ref = next(it) if has_bias else None
    if has_res:
        x_ref, ga_ref = next(it), next(it)
    o_ref, h_ref = next(it), next(it)

    @pl.when(pl.program_id(2) == 0)
    def _():
        a = a_refs[0][0]
        for r in a_refs[1:]:
            a = a + r[0]
        if prologue == "normmod":
            a = _normmod(a, g_ref[...], sc_ref[0], sh_ref[0])
        elif prologue == "silu":
            a = a / (1.0 + jnp.exp(-a))
        h_ref[...] = a.astype(BF16)

    y = jnp.dot(h_ref[...], w_ref[...], preferred_element_type=F32)
    if has_bias:
        y = y + b_ref[...]
    if has_res:
        y = x_ref[0] + ga_ref[0] * y
    o_ref[0] = y.astype(o_ref.dtype)


def _mm(a_list, w, *, tm, tn=None, norm=None, prologue=None, bias=None, res=None, name="mm"):
    B, S, K = a_list[0].shape
    N = w.shape[1]
    tn = N if tn is None else tn
    grid = (B, S // tm, N // tn)
    ins, specs = [], []
    for a in a_list:
        ins.append(a)
        specs.append(pl.BlockSpec((1, tm, K), lambda b, j, n: (b, j, 0)))

    def mod_spec(m, width, tiled_n):
        if m.shape[1] != 1:
            return pl.BlockSpec((1, tm, width), (lambda b, j, n: (0, j, n)) if tiled_n else (lambda b, j, n: (0, j, 0)))
        return pl.BlockSpec((1, 1, width), (lambda b, j, n: (b, 0, n)) if tiled_n else (lambda b, j, n: (b, 0, 0)))

    if norm is not None:
        g, sc, sh = norm
        prologue = "normmod"
        ins += [g, sc, sh]
        specs += [pl.BlockSpec((1, K), lambda b, j, n: (0, 0)), mod_spec(sc, K, False), mod_spec(sh, K, False)]
    ins.append(w)
    specs.append(pl.BlockSpec((K, tn), lambda b, j, n: (0, n)))
    if bias is not None:
        ins.append(bias)
        specs.append(pl.BlockSpec((1, tn), lambda b, j, n: (0, n)))
    if res is not None:
        x, ga = res
        ins += [x, ga]
        specs += [pl.BlockSpec((1, tm, tn), lambda b, j, n: (b, j, n)), mod_spec(ga, tn, True)]
    return pl.pallas_call(
        functools.partial(_mm_kernel, n_in=len(a_list), prologue=prologue,
                          has_bias=bias is not None, has_res=res is not None),
        grid=grid, in_specs=specs,
        out_specs=pl.BlockSpec((1, tm, tn), lambda b, j, n: (b, j, n)),
        out_shape=jax.ShapeDtypeStruct((B, S, N), F32),
        scratch_shapes=[pltpu.VMEM((tm, K), BF16)],
        compiler_params=_cparams(("parallel", "parallel", "arbitrary")),
        name=name,
    )(*ins)


def _rmsnorm_kernel(x_ref, g_ref, o_ref):
    x = x_ref[0]
    r = lax.rsqrt(jnp.mean(x * x, axis=-1, keepdims=True) + RMS_EPS)
    o_ref[0] = (x * r) * g_ref[...]


def _rmsnorm(x, g, *, tm):
    B, S, D = x.shape
    return pl.pallas_call(
        _rmsnorm_kernel, grid=(B, S // tm),
        in_specs=[pl.BlockSpec((1, tm, D), lambda b, j: (b, j, 0)), pl.BlockSpec((1, D), lambda b, j: (0, 0))],
        out_specs=pl.BlockSpec((1, tm, D), lambda b, j: (b, j, 0)),
        out_shape=jax.ShapeDtypeStruct((B, S, D), F32),
        compiler_params=_cparams(("parallel", "parallel")), name="final_norm",
    )(x, g)


def _chunk_proj_steps(x_ref, w1_ref, n_chunks, p_ref):
    n_steps = CMP_STRIDE // 2
    state = {}

    def step(s2):
        l0 = x_ref[pl.ds(2 * s2, n_chunks, stride=CMP_STRIDE), :]
        l1 = x_ref[pl.ds(2 * s2 + 1, n_chunks, stride=CMP_STRIDE), :]
        lhs = jnp.concatenate([l0, l1], axis=-1).astype(BF16)
        d = jnp.dot(lhs, w1_ref[0, s2], preferred_element_type=F32)
        state["acc"] = d if s2 == 0 else state["acc"] + d
        if s2 == n_steps - 1:
            p_ref[0:n_chunks, :] = state.pop("acc")
    return [functools.partial(step, s2) for s2 in range(n_steps)]


def _chunk_proj(x_ref, w1_ref, n_chunks, p_ref):
    for step in _chunk_proj_steps(x_ref, w1_ref, n_chunks, p_ref):
        step()


def _cmp_finish(p_ref, nb, b1_ref, w2T_ref, o_ref, g_base=0):
    for gp in range(2):
        c0 = gp * 2 * CMP_HID
        acc = p_ref[0:nb, c0:c0 + CMP_HID] + p_ref[1:nb + 1, c0 + CMP_HID:c0 + 2 * CMP_HID]
        hid = _gelu(acc + b1_ref[0]).astype(BF16)
        kT = lax.dot_general(w2T_ref[0], hid, (((1,), (1,)), ((), ())), preferred_element_type=F32)
        o_ref[0, 0, g_base + gp] = kT.astype(o_ref.dtype)


def _cmp_prompt_kernel(x_ref, w1_ref, b1_ref, w2T_ref, o_ref, p_ref, *, n_chunks, nbp):
    p_ref[n_chunks:, :] = jnp.zeros((p_ref.shape[0] - n_chunks, p_ref.shape[1]), F32)
    _chunk_proj(x_ref.at[0], w1_ref, n_chunks, p_ref)
    _cmp_finish(p_ref, nbp, b1_ref, w2T_ref, o_ref)


def _cmp_sample_kernel(pt_ref, *refs, n_pages, n_chunks):
    page_refs = refs[:n_pages]
    xn_ref, w1_ref, b1_ref, w2T_ref, o_ref, xs_ref, p_ref = refs[n_pages:]

    def transposes(gp):
        per = max(1, n_pages * 2 // CMP_STRIDE)

        def some(p0):
            for p in range(p0, min(p0 + per, n_pages)):
                xs_ref[gp, p * PAGE_SIZE:(p + 1) * PAGE_SIZE, :] = page_refs[p][0, 0, gp].T
        return [functools.partial(some, p0) for p0 in range(0, n_pages, per)]

    def finish(gp):
        pg = p_ref.at[gp]
        new = jnp.dot(xn_ref[0, :, gp * LANES:(gp + 1) * LANES].astype(BF16), w1_ref[0, 0, 0:LANES, :],
                      preferred_element_type=F32)
        pg[n_chunks:n_chunks + SUBLANES, :] = new
        _cmp_finish(pg, n_chunks, b1_ref, w2T_ref, o_ref, g_base=2 * gp)

    for t in transposes(0):
        t()
    t1 = transposes(1)
    p0 = _chunk_proj_steps(xs_ref.at[0], w1_ref, n_chunks, p_ref.at[0])
    for k in range(max(len(t1), len(p0))):
        if k < len(t1):
            t1[k]()
        if k < len(p0):
            p0[k]()
    finish(0)
    _chunk_proj(xs_ref.at[1], w1_ref, n_chunks, p_ref.at[1])
    finish(1)


def _cmp_weights(w1, w2):
    w1r = w1.reshape(2, 2, CMP_STRIDE, HEAD_DIM, CMP_HID).transpose(0, 2, 3, 1, 4)
    w1r = w1r.reshape(2, CMP_STRIDE, HEAD_DIM, 2 * CMP_HID)
    eye = jnp.eye(2, dtype=F32)
    wp = jnp.einsum("ab,ksdn->ksadbn", eye, w1r).reshape(2, CMP_STRIDE // 2, 4 * HEAD_DIM, 4 * CMP_HID)
    return wp.astype(BF16), w2.transpose(0, 2, 1).astype(BF16)


def _cmp_prompt(kv_c, w1p, b1, w2T):
    B, S, _ = kv_c.shape
    n_chunks = S // CMP_STRIDE
    nbp = n_chunks
    return pl.pallas_call(
        functools.partial(_cmp_prompt_kernel, n_chunks=n_chunks, nbp=nbp),
        grid=(B, 2, 2),
        in_specs=[
            pl.BlockSpec((1, S, LANES), lambda b, k, gp: (b, 0, k * 2 + gp)),
            pl.BlockSpec((1, CMP_STRIDE // 2, 2 * LANES, 4 * CMP_HID), lambda b, k, gp: (k, 0, 0, 0)),
            pl.BlockSpec((1, 1, CMP_HID), lambda b, k, gp: (k, 0, 0)),
            pl.BlockSpec((1, HEAD_DIM, CMP_HID), lambda b, k, gp: (k, 0, 0)),
        ],
        out_specs=pl.BlockSpec((1, 1, 2, HEAD_DIM, nbp), lambda b, k, gp: (b, k, gp, 0, 0)),
        out_shape=jax.ShapeDtypeStruct((B, 2, G_A, HEAD_DIM, nbp), BF16),
        scratch_shapes=[pltpu.VMEM((n_chunks + SUBLANES, 4 * CMP_HID), F32)],
        compiler_params=_cparams(("parallel", "parallel", "parallel")), name="cmp_prompt",
    )(kv_c, w1p, b1, w2T)


def _cmp_sample(cacheT, page_table, kv_c_new, w1p, b1, w2T):
    Bd, n_pages = page_table.shape
    n_chunks = n_pages * PAGE_SIZE // CMP_STRIDE

    def page_spec(p):
        return pl.BlockSpec((1, 1, 2, LANES, PAGE_SIZE), lambda b, k, pt: (pt[b, p], k, 0, 0, 0))

    return pl.pallas_call(
        functools.partial(_cmp_sample_kernel, n_pages=n_pages, n_chunks=n_chunks),
        grid_spec=pltpu.PrefetchScalarGridSpec(
            num_scalar_prefetch=1, grid=(Bd, 2),
            in_specs=[page_spec(p) for p in range(n_pages)] + [
                pl.BlockSpec((1, SUBLANES, 2 * LANES), lambda b, k, pt: (b, 0, k)),
                pl.BlockSpec((1, CMP_STRIDE // 2, 2 * LANES, 4 * CMP_HID), lambda b, k, pt: (k, 0, 0, 0)),
                pl.BlockSpec((1, 1, CMP_HID), lambda b, k, pt: (k, 0, 0)),
                pl.BlockSpec((1, HEAD_DIM, CMP_HID), lambda b, k, pt: (k, 0, 0)),
            ],
            out_specs=pl.BlockSpec((1, 1, G_A, HEAD_DIM, n_chunks), lambda b, k, pt: (b, k, 0, 0, 0)),
            scratch_shapes=[pltpu.VMEM((2, n_pages * PAGE_SIZE, LANES), F32),
                            pltpu.VMEM((2, n_chunks + SUBLANES, 4 * CMP_HID), F32)],
        ),
        out_shape=jax.ShapeDtypeStruct((Bd, 2, G_A, HEAD_DIM, n_chunks), BF16),
        compiler_params=_cparams(("parallel", "parallel")), name="cmp_sample",
    )(page_table, *([cacheT] * n_pages), kv_c_new, w1p, b1, w2T)


_LOWEST = -3.0e38


def _select_rounds(score, lane):
    picks = []
    for _ in range(N_SEL):
        m = jnp.max(score, axis=-1, keepdims=True)
        idx = jnp.min(jnp.where(score == m, lane, 1 << 20), axis=-1, keepdims=True)
        picks.append(idx)
        score = jnp.where(lane == idx, _LOWEST, score)
    return picks


def _overlap_matrix(nb_pad, ns_pad, nb, ns):
    cs = jnp.arange(nb_pad) * CMP_STRIDE
    ss = jnp.arange(ns_pad) * SEL_BLOCK
    ov = jnp.maximum(jnp.minimum(cs[:, None] + CMP_LEN, ss[None, :] + SEL_BLOCK)
                     - jnp.maximum(cs[:, None], ss[None, :]), 0).astype(F32) / CMP_LEN
    ov = jnp.where((jnp.arange(nb_pad)[:, None] < nb) & (jnp.arange(ns_pad)[None, :] < ns), ov, 0.0)
    return ov.astype(BF16)


def _row_info(rows, tq, q0):
    row = lax.broadcasted_iota(jnp.int32, (rows, 1), 0)
    return q0 + (row & (tq - 1))


def _cmp_attend_kernel(q_ref, kT_ref, vT_ref, slope_ref, gate_ref, ov_ref, o_ref, bits_ref, *, tq, nb, ns):
    R = q_ref.shape[2]
    rows = R * tq
    q0 = pl.program_id(2) * tq
    q = q_ref[0, 0].reshape(rows, HEAD_DIM)
    qpos = _row_info(rows, tq, q0)
    slope = slope_ref[0]
    nbp = kT_ref.shape[-1]
    blk_end = lax.broadcasted_iota(jnp.int32, (1, nbp), 1) * CMP_STRIDE + (CMP_LEN - 1)
    s = jnp.dot(q, kT_ref[0, 0, 0], preferred_element_type=F32)
    s = s + slope * (blk_end - q0).astype(F32)
    valid = (blk_end <= qpos) & (blk_end < nb * CMP_STRIDE + CMP_LEN - 1)
    s = jnp.where(valid, s, NEG)
    m = jnp.max(s, axis=-1, keepdims=True)
    e = jnp.exp(s - m)
    p = e * (jnp.where(qpos >= CMP_LEN - 1, 1.0, 0.0) / jnp.sum(e, axis=-1, keepdims=True))
    pb = p.astype(BF16)
    o = lax.dot_general(pb, vT_ref[0, 0, 0], (((1,), (1,)), ((), ())), preferred_element_type=F32)
    imp_r = jnp.dot(pb, ov_ref[...], preferred_element_type=F32)
    g = gate_ref[0, 0]
    imp = None
    for r in range(R):
        gate = 1.0 / (1.0 + jnp.exp(-g[:, r:r + 1]))
        o_ref[0, 0, r] = o[r * tq:(r + 1) * tq] * gate
        part = imp_r[r * tq:(r + 1) * tq]
        imp = part if imp is None else imp + part
    tpos = qpos[0:tq]
    lane = lax.broadcasted_iota(jnp.int32, (1, imp.shape[1]), 1)
    cur = tpos >> 6
    forced = (lane == 0) | (lane == cur) | (lane == cur - 1)
    score = jnp.where(lane * SEL_BLOCK <= tpos, imp + jnp.where(forced, FORCE, 0.0), NEG)
    score = jnp.where(lane < ns, score, _LOWEST)
    bits = jnp.zeros((tq, 1), jnp.int32)
    for idx in _select_rounds(score, lane):
        bits = bits | (1 << idx)
    bits_ref[0, 0] = jnp.transpose(jnp.broadcast_to(bits, (tq, LANES)))[0:SUBLANES, :]


def _cmp_attend_prompt(q_t, kvcT, slope_rows, gates_t, ov, *, tq, nb, ns):
    B, G, R, S, _ = q_t.shape
    nbp = kvcT.shape[-1]
    return pl.pallas_call(
        functools.partial(_cmp_attend_kernel, tq=tq, nb=nb, ns=ns),
        grid=(B, G, S // tq),
        in_specs=[
            pl.BlockSpec((1, 1, R, tq, HEAD_DIM), lambda b, g, i: (b, g, 0, i, 0)),
            pl.BlockSpec((1, 1, 1, HEAD_DIM, nbp), lambda b, g, i: (b, 0, g, 0, 0)),
            pl.BlockSpec((1, 1, 1, HEAD_DIM, nbp), lambda b, g, i: (b, 1, g, 0, 0)),
            pl.BlockSpec((1, R * tq, 1), lambda b, g, i: (g, 0, 0)),
            pl.BlockSpec((1, 1, tq, 16), lambda b, g, i: (b, g, i, 0)),
            pl.BlockSpec(ov.shape, lambda b, g, i: (0, 0)),
        ],
        out_specs=[
            pl.BlockSpec((1, 1, R, tq, HEAD_DIM), lambda b, g, i: (b, g, 0, i, 0)),
            pl.BlockSpec((1, 1, SUBLANES, tq), lambda b, g, i: (b, g, 0, i)),
        ],
        out_shape=[jax.ShapeDtypeStruct((B, G, R, S, HEAD_DIM), F32),
                   jax.ShapeDtypeStruct((B, G, SUBLANES, S), jnp.int32)],
        compiler_params=_cparams(("parallel", "parallel", "parallel")), name="cmp_attend_prompt",
    )(q_t, kvcT, kvcT, slope_rows, gates_t, ov)


_QK_FEAT = 16


def _flash_kernel(*refs, tq, tk, window, use_bits, gate_row, use_sink):
    it = iter(refs)
    qT_ref, k_ref, vT_ref = next(it), next(it), next(it)
    bits_ref = next(it) if use_bits else None
    gate_ref = next(it) if gate_row is not None else None
    if use_sink:
        sink_ref, slope_ref = next(it), next(it)
    o_ref = next(it)
    m_ref, l_ref, acc_ref = next(it), next(it), next(it)
    R = qT_ref.shape[2]
    qi = pl.program_id(2)
    q0 = qi * tq
    tpos = q0 + lax.broadcasted_iota(jnp.int32, (1, tq), 1)
    if use_bits:
        bits = bits_ref[0, 0, 0:1, :]
    hi = (q0 + tq + tk - 1) // tk
    lo = 0 if window is None else jnp.maximum(q0 - (window - 1), 0) // tk
    m_ref[...] = jnp.full(m_ref.shape, NEG, F32)
    l_ref[...] = jnp.zeros(l_ref.shape, F32)
    acc_ref[...] = jnp.zeros(acc_ref.shape, F32)

    def body(j, carry):
        k0 = pl.multiple_of(j * tk, tk)
        kt = k_ref[0, 0, pl.ds(k0, tk), :]
        vT = vT_ref[0, 0, :, pl.ds(k0, tk)]
        kpos = k0 + lax.broadcasted_iota(jnp.int32, (tk, 1), 0)
        mask = kpos <= tpos
        if window is not None:
            mask = mask & (kpos > tpos - window)
        if use_bits:
            mask = mask & ((bits & (1 << (kpos >> 6))) != 0)
        mbias = jnp.where(mask, 0.0, NEG)
        hs = range(R)
        ss = [jnp.dot(kt, qT_ref[0, 0, r], preferred_element_type=F32) + mbias for r in hs]
        m_olds = [m_ref[r] for r in hs]
        m_news = [jnp.maximum(m_olds[r], jnp.max(ss[r], axis=0, keepdims=True)) for r in hs]
        alphas = [jnp.exp(m_olds[r] - m_news[r]) for r in hs]
        ps = [jnp.exp(ss[r] - m_news[r]) for r in hs]
        sums = [jnp.sum(ps[r], axis=0, keepdims=True) for r in hs]
        pvs = [jnp.dot(vT, ps[r].astype(BF16), preferred_element_type=F32) for r in hs]
        for r in hs:
            l_ref[r] = alphas[r] * l_ref[r] + sums[r]
            acc_ref[r] = alphas[r] * acc_ref[r] + pvs[r]
            m_ref[r] = m_news[r]
        return carry

    lax.fori_loop(lo, hi, body, 0)
    for r in range(R):
        m, l, acc = m_ref[r], l_ref[r], acc_ref[r]
        if use_sink:
            sk = sink_ref[0, r] + slope_ref[0, r] * tpos.astype(F32)
            m_new = jnp.maximum(m, sk)
            alpha = jnp.exp(m - m_new)
            l = alpha * l + jnp.exp(sk - m_new)
            acc = alpha * acc
        scale = 1.0 / l
        if gate_row is not None:
            c = gate_row * R + r
            scale = scale * (1.0 / (1.0 + jnp.exp(-gate_ref[0, 0, c:c + 1, :])))
        o_ref[0, 0, r] = acc * scale


def _flash(qT, k, vT, *, tq, tk, window=None, bits=None, gates_t=None, gate_row=None,
           sink_t=None, slope_t=None, name="flash"):
    B, G, R, KD, S = qT.shape
    ins = [qT, k, vT]
    specs = [
        pl.BlockSpec((1, 1, R, KD, tq), lambda b, g, i: (b, g, 0, 0, i)),
        pl.BlockSpec((1, 1, S, KD), lambda b, g, i: (b, g, 0, 0)),
        pl.BlockSpec((1, 1, HEAD_DIM, S), lambda b, g, i: (b, g, 0, 0)),
    ]
    if bits is not None:
        ins.append(bits)
        specs.append(pl.BlockSpec((1, 1, SUBLANES, tq), lambda b, g, i: (b, g, 0, i)))
    if gates_t is not None:
        ins.append(gates_t)
        specs.append(pl.BlockSpec((1, 1, 16, tq), lambda b, g, i: (b, g, 0, i)))
    if sink_t is not None:
        ins += [sink_t, slope_t]
        specs += [pl.BlockSpec((1, R, 1, tq), lambda b, g, i: (g, 0, 0, 0))] * 2
    return pl.pallas_call(
        functools.partial(_flash_kernel, tq=tq, tk=tk, window=window, use_bits=bits is not None,
                          gate_row=gate_row if gates_t is not None else None, use_sink=sink_t is not None),
        grid=(B, G, S // tq), in_specs=specs,
        out_specs=pl.BlockSpec((1, 1, R, HEAD_DIM, tq), lambda b, g, i: (b, g, 0, 0, i)),
        out_shape=jax.ShapeDtypeStruct((B, G, R, HEAD_DIM, S), F32),
        scratch_shapes=[pltpu.VMEM((R, 1, tq), F32), pltpu.VMEM((R, 1, tq), F32),
                        pltpu.VMEM((R, HEAD_DIM, tq), F32)],
        compiler_params=_cparams(("parallel", "parallel", "parallel")), name=name,
    )(*ins)


def _softmax_parts(s, valid, extra_s, extra_valid, sink):
    s = jnp.where(valid, s, NEG)
    m = jnp.max(s, axis=-1, keepdims=True)
    if extra_s is not None:
        extra_s = jnp.where(extra_valid, extra_s, NEG)
        m = jnp.maximum(m, extra_s)
    if sink is not None:
        m = jnp.maximum(m, sink)
    e = jnp.exp(s - m)
    l = jnp.sum(e, axis=-1, keepdims=True)
    e_new = None
    if extra_s is not None:
        e_new = jnp.where(extra_valid, jnp.exp(extra_s - m), 0.0)
        l = l + e_new
    if sink is not None:
        l = l + jnp.exp(sink - m)
    return e, e_new, l


def _new_key_score(q16, k_new):
    return jnp.sum(q16.astype(F32) * k_new.astype(BF16).astype(F32), axis=-1, keepdims=True)


def _dec_cmp_kernel(q_ref, kT_ref, vT_ref, slope_ref, gate_ref, ov_ref, o_ref, idx_ref, *, qpos, n_past_blocks):
    q16 = q_ref[0]
    slope = slope_ref[...]
    nb = kT_ref.shape[-1]
    nsp = ov_ref.shape[1]
    blk_end = lax.broadcasted_iota(jnp.int32, (1, nb), 1) * CMP_STRIDE + (CMP_LEN - 1)
    valid = blk_end <= qpos
    bias = slope * (blk_end - qpos).astype(F32)
    row = lax.broadcasted_iota(jnp.int32, (H_A, 1), 0)
    lane = lax.broadcasted_iota(jnp.int32, (1, nsp), 1)
    cur = qpos // SEL_BLOCK
    forced = (lane == 0) | (lane == cur) | (lane == cur - 1)
    ns = n_past_blocks + 1
    o = jnp.zeros((H_A, HEAD_DIM), F32)
    tile_l = lax.broadcasted_iota(jnp.int32, (SUBLANES, LANES), 1)
    row8 = lax.broadcasted_iota(jnp.int32, (SUBLANES, 1), 0)
    score8 = jnp.full((SUBLANES, nsp), _LOWEST, F32)
    gs = range(G_A)
    ss = [jnp.where(valid, jnp.dot(q16, kT_ref[0, 0, g], preferred_element_type=F32) + bias, NEG) for g in gs]
    ms = [jnp.max(ss[g], axis=-1, keepdims=True) for g in gs]
    es = [jnp.exp(ss[g] - ms[g]) for g in gs]
    ls = [jnp.sum(es[g], axis=-1, keepdims=True) for g in gs]
    pbs = [(es[g] / ls[g]).astype(BF16) for g in gs]
    ogs = [lax.dot_general(pbs[g], vT_ref[0, 0, g], (((1,), (1,)), ((), ())), preferred_element_type=F32)
           for g in gs]
    imps = [jnp.dot(pbs[g], ov_ref[...], preferred_element_type=F32) for g in gs]
    for g in gs:
        mine = (row >> 2) == g
        o = jnp.where(mine, ogs[g], o)
        imp = jnp.sum(jnp.where(mine, imps[g], 0.0), axis=0, keepdims=True)
        score = jnp.where(lane * SEL_BLOCK <= qpos, imp + jnp.where(forced, FORCE, 0.0), NEG)
        score = jnp.where(lane < ns, score, _LOWEST)
        score8 = jnp.where(row8 == g, score, score8)
    tile = jnp.zeros((SUBLANES, LANES), jnp.int32)
    for r, idx in enumerate(_select_rounds(score8, lane)):
        tile = jnp.where(tile_l == r, idx, tile)
    gate = 1.0 / (1.0 + jnp.exp(-gate_ref[0][:, 0:1]))
    o_ref[0] = o * gate
    idx_ref[0] = tile


def _dec_cmp(q16, kvcT, slope16, gates16, ov, *, qpos, n_past_blocks):
    Bd = q16.shape[0]
    nb = kvcT.shape[-1]
    return pl.pallas_call(
        functools.partial(_dec_cmp_kernel, qpos=qpos, n_past_blocks=n_past_blocks),
        grid=(Bd,),
        in_specs=[
            pl.BlockSpec((1, H_A, HEAD_DIM), lambda b: (b, 0, 0)),
            pl.BlockSpec((1, 1, G_A, HEAD_DIM, nb), lambda b: (b, 0, 0, 0, 0)),
            pl.BlockSpec((1, 1, G_A, HEAD_DIM, nb), lambda b: (b, 1, 0, 0, 0)),
            pl.BlockSpec((H_A, 1), lambda b: (0, 0)),
            pl.BlockSpec((1, H_A, 4), lambda b: (b, 0, 0)),
            pl.BlockSpec(ov.shape, lambda b: (0, 0)),
        ],
        out_specs=[pl.BlockSpec((1, H_A, HEAD_DIM), lambda b: (b, 0, 0)),
                   pl.BlockSpec((1, SUBLANES, LANES), lambda b: (b, 0, 0))],
        out_shape=[jax.ShapeDtypeStruct((Bd, H_A, HEAD_DIM), F32),
                   jax.ShapeDtypeStruct((Bd, SUBLANES, LANES), jnp.int32)],
        compiler_params=_cparams(("parallel",)), name="dec_cmp",
    )(q16, kvcT, kvcT, slope16, gates16, ov)


def _dec_sel_kernel(idx_ref, pt_ref, *refs, qpos, n_past_blocks):
    kv_refs = refs[:G_A * N_SEL]
    q_ref, kvn_ref, slope_ref, gate_ref, o_ref = refs[G_A * N_SEL:]
    b = pl.program_id(0)
    q16 = q_ref[0]
    slope = slope_ref[...]
    lane = lax.broadcasted_iota(jnp.int32, (1, PAGE_SIZE), 1)
    row = lax.broadcasted_iota(jnp.int32, (H_A, 1), 0)
    o = jnp.zeros((H_A, HEAD_DIM), F32)
    for g in range(G_A):
        s_l, valid_l, v_l = [], [], []
        any_new = None
        for kk in range(N_SEL):
            idx = idx_ref[(b * G_A + g) * N_SEL + kk]
            is_new = (jnp.zeros((H_A, 1), jnp.int32) + idx) >= n_past_blocks
            any_new = is_new if any_new is None else (any_new | is_new)
            jp = jnp.minimum(idx, n_past_blocks - 1)
            kpos = (jp >> 1) * PAGE_SIZE + lane
            tile = kv_refs[g * N_SEL + kk]
            s = jnp.dot(q16, tile[0, 0, 0].astype(BF16), preferred_element_type=F32)
            s_l.append(s - slope * (qpos - kpos).astype(F32))
            in_block = ((lane >> 6) == (jp & 1)) & ((lane * 0 + idx) < n_past_blocks) & (kpos <= qpos)
            valid_l.append(jnp.broadcast_to(in_block, (H_A, PAGE_SIZE)))
            v_l.append(tile[0, 1, 0].astype(BF16))
        s = jnp.concatenate(s_l, axis=-1)
        valid = jnp.concatenate(valid_l, axis=-1)
        vT = jnp.concatenate(v_l, axis=-1)
        s_new = _new_key_score(q16, kvn_ref[0, 0, g])
        e, e_new, l = _softmax_parts(s, valid, s_new, any_new, None)
        og = lax.dot_general(e.astype(BF16), vT, (((1,), (1,)), ((), ())), preferred_element_type=F32)
        og = (og + e_new * kvn_ref[0, 1, g].astype(BF16).astype(F32)) / l
        o = jnp.where((row >> 2) == g, og, o)
    gate = 1.0 / (1.0 + jnp.exp(-gate_ref[0][:, 1:2]))
    o_ref[0] = o * gate


def _dec_sel(idx_flat, page_table, cacheT, q16, kv_new, slope16, gates16, *, qpos, n_past_blocks):
    Bd = q16.shape[0]

    def tile_spec(g, kk):
        def imap(b, idx, pt):
            jp = jnp.minimum(idx[(b * G_A + g) * N_SEL + kk], n_past_blocks - 1)
            return (pt[b, jp >> 1], 0, g, 0, 0)
        return pl.BlockSpec((1, 2, 1, HEAD_DIM, PAGE_SIZE), imap)

    return pl.pallas_call(
        functools.partial(_dec_sel_kernel, qpos=qpos, n_past_blocks=n_past_blocks),
        grid_spec=pltpu.PrefetchScalarGridSpec(
            num_scalar_prefetch=2, grid=(Bd,),
            in_specs=[tile_spec(g, kk) for g in range(G_A) for kk in range(N_SEL)] + [
                pl.BlockSpec((1, H_A, HEAD_DIM), lambda b, idx, pt: (b, 0, 0)),
                pl.BlockSpec((1, 2, G_A, 1, HEAD_DIM), lambda b, idx, pt: (b, 0, 0, 0, 0)),
                pl.BlockSpec((H_A, 1), lambda b, idx, pt: (0, 0)),
                pl.BlockSpec((1, H_A, 4), lambda b, idx, pt: (b, 0, 0)),
            ],
            out_specs=pl.BlockSpec((1, H_A, HEAD_DIM), lambda b, idx, pt: (b, 0, 0)),
        ),
        out_shape=jax.ShapeDtypeStruct((Bd, H_A, HEAD_DIM), F32),
        compiler_params=_cparams(("parallel",)), name="dec_sel",
    )(idx_flat, page_table, *([cacheT] * (G_A * N_SEL)), q16, kv_new, slope16, gates16)


def _dec_slab_kernel(*refs, n_groups, window, qpos, gate_col, use_sink):
    it = iter(refs)
    q_ref, kT_ref, vT_ref, kn_ref, vn_ref, slope_ref = (next(it) for _ in range(6))
    gate_ref = next(it) if gate_col is not None else None
    sink_ref = next(it) if use_sink else None
    o_ref = next(it)
    q16 = q_ref[0]
    slope = slope_ref[...]
    nbuf = kT_ref.shape[-1]
    rpg = q16.shape[0] // n_groups
    kpos = qpos - nbuf + lax.broadcasted_iota(jnp.int32, (1, nbuf), 1)
    dist = qpos - kpos
    valid = (dist >= 0) & (dist < window) & (kpos >= 0)
    bias = -slope * dist.astype(F32)
    row = lax.broadcasted_iota(jnp.int32, (q16.shape[0], 1), 0)
    sink = sink_ref[...] if use_sink else None
    o = jnp.zeros((q16.shape[0], HEAD_DIM), F32)
    for g in range(n_groups):
        s = jnp.dot(q16, kT_ref[0, 0, g].astype(BF16), preferred_element_type=F32) + bias
        s_new = _new_key_score(q16, kn_ref[0, 0, g:g + 1, :])
        e, e_new, l = _softmax_parts(s, valid, s_new, True, sink)
        og = lax.dot_general(e.astype(BF16), vT_ref[0, 0, g].astype(BF16), (((1,), (1,)), ((), ())),
                             preferred_element_type=F32)
        og = (og + e_new * vn_ref[0, 0, g:g + 1, :].astype(BF16).astype(F32)) / l
        o = jnp.where((row // rpg) == g, og, o)
    if gate_col is not None:
        o = o * (1.0 / (1.0 + jnp.exp(-gate_ref[0][:, gate_col:gate_col + 1])))
    o_ref[0] = o


def _dec_slab(q16, slabT, kv_new, slope16, *, window, qpos, gates16=None, gate_col=None, sink16=None, name):
    Bd, _, G, _, nbuf = slabT.shape
    H = q16.shape[1]
    ins = [q16, slabT, slabT, kv_new, kv_new, slope16]
    specs = [
        pl.BlockSpec((1, H, HEAD_DIM), lambda b: (b, 0, 0)),
        pl.BlockSpec((1, 1, G, HEAD_DIM, nbuf), lambda b: (b, 0, 0, 0, 0)),
        pl.BlockSpec((1, 1, G, HEAD_DIM, nbuf), lambda b: (b, 1, 0, 0, 0)),
        pl.BlockSpec((1, 1, G, HEAD_DIM), lambda b: (b, 0, 0, 0)),
        pl.BlockSpec((1, 1, G, HEAD_DIM), lambda b: (b, 1, 0, 0)),
        pl.BlockSpec((H, 1), lambda b: (0, 0)),
    ]
    if gates16 is not None:
        ins.append(gates16)
        specs.append(pl.BlockSpec((1, H, 4), lambda b: (b, 0, 0)))
    if sink16 is not None:
        ins.append(sink16)
        specs.append(pl.BlockSpec((H, 1), lambda b: (0, 0)))
    return pl.pallas_call(
        functools.partial(_dec_slab_kernel, n_groups=G, window=window, qpos=qpos,
                          gate_col=gate_col if gates16 is not None else None, use_sink=sink16 is not None),
        grid=(Bd,), in_specs=specs,
        out_specs=pl.BlockSpec((1, H, HEAD_DIM), lambda b: (b, 0, 0)),
        out_shape=jax.ShapeDtypeStruct((Bd, H, HEAD_DIM), F32),
        compiler_params=_cparams(("parallel",)), name=name,
    )(*ins)


def _alibi(n_heads):
    return 2.0 ** (-8.0 * jnp.arange(1, n_heads + 1, dtype=F32) / n_heads)


def _heads_first(z, G, R):
    B, S, _ = z.shape
    return z.reshape(B, S, G, R, HEAD_DIM).transpose(0, 2, 3, 1, 4)


def _heads_last(o):
    B, G, R, S, _ = o.shape
    return o.transpose(0, 3, 1, 2, 4).reshape(B, S, G * R * HEAD_DIM)


def kernel(x_prompt, x_sample, c_prompt, c_sample, cache_cmp_kv, cache_sel_kv, state_win_kv, state_shared_kv, page_table, w_ada, b_ada, g_norm, w_in_a, w_o_a, cmp_pos, w_cmp1, b_cmp1, w_cmp2, g_kv, w_ada_kv, b_ada_kv, w_kv_b, w_q_b, w_o_b, sinks, w_pq, peer_keys, peer_u, peer_v, g_final):
    B, S, D = x_prompt.shape
    Bd = x_sample.shape[0]
    n_pages = page_table.shape[1]
    past_len = n_pages * PAGE_SIZE
    n_past_blocks = past_len // SEL_BLOCK
    TQ = 256
    TQ_CMP = 512
    bf = lambda a: a.astype(BF16)

    c_all = jnp.concatenate([c_prompt, c_sample], axis=0)[None]
    nc = B + Bd

    def ada(w, b):
        return _mm([c_all], bf(w), tm=nc, tn=2048, prologue="silu", bias=b[None], name="ada")[0]

    mods = [ada(w_ada[i], b_ada[i]) for i in range(2)]
    mods_kv = ada(w_ada_kv, b_ada_kv)

    def split_mods(m, n):
        parts = jnp.split(m, n, axis=-1)
        return [p[:B, None, :] for p in parts], [p[None, B:, :] for p in parts]

    mp0, ms0 = split_mods(mods[0], 6)
    mp1, ms1 = split_mods(mods[1], 6)
    mpk, msk = split_mods(mods_kv, 2)

    w_in = bf(jnp.pad(w_in_a[0], ((0, 0), (0, 21 * LANES - NSA_IN))))
    w1p, w2T = _cmp_weights(w_cmp1[0], w_cmp2[0])
    pos_rows = jnp.pad(cmp_pos[0].reshape(2, 1, CMP_LEN * HEAD_DIM), ((0, 0), (0, SUBLANES - 1), (0, 0)))
    b1 = jnp.stack([_mm([pos_rows[k:k + 1]], bf(w_cmp1[0, k]), tm=SUBLANES, bias=b_cmp1[0, k][None],
                        name="cmp_bias")[0, 0:1] for k in range(2)])
    slopes_a = _alibi(H_A)
    slopes_b = _alibi(H_B)
    gn = lambda v: v.reshape(1, D)
    peer_w = [(bf(w_pq[i].T), bf(peer_keys[i].reshape(2 * PEER_HEADS, N_KEYS, N_KEYS)), bf(peer_u[i]), bf(peer_v[i]))
              for i in range(2)]

    def rows_of(v, G, R, tq=TQ):
        return jnp.repeat(v.reshape(G, R), tq, axis=1)[..., None]

    z = _mm([x_prompt], w_in, tm=512, norm=(gn(g_norm[0, 0]), mp0[1], mp0[0]), name="nsa_in_p")
    kv_c, kv_s, kv_w = (z[..., NSA_Q + i * NSA_KV:NSA_Q + (i + 1) * NSA_KV] for i in range(3))
    q_t = bf(_heads_first(z[..., :NSA_Q] * (HEAD_DIM ** -0.5), G_A, R_A))
    glog = z[..., NSA_Q + 3 * NSA_KV:NSA_IN].reshape(B, S, 3, G_A, R_A).transpose(0, 3, 1, 2, 4)
    gates_t = jnp.pad(glog.reshape(B, G_A, S, 3 * R_A), ((0, 0), (0, 0), (0, 0), (0, 16 - 3 * R_A)))

    kvcT = _cmp_prompt(kv_c, w1p, b1, w2T)
    nb_p = S // CMP_STRIDE - 1
    ns_p = S // SEL_BLOCK
    ov_p = _overlap_matrix(kvcT.shape[-1], LANES, nb_p, ns_p)
    o_c, bits = _cmp_attend_prompt(q_t, kvcT, rows_of(slopes_a, G_A, R_A, TQ_CMP), gates_t, ov_p,
                                   tq=TQ_CMP, nb=nb_p, ns=ns_p)

    kpos = jnp.arange(S)
    kp_hi, kp_lo = ((kpos >> 6) * SEL_BLOCK).astype(F32), (kpos & (SEL_BLOCK - 1)).astype(F32)
    feat_k = bf(jnp.pad(jnp.stack([kp_hi, kp_lo, kp_hi, kp_lo], axis=-1), ((0, 0), (0, _QK_FEAT - 4))))

    def split_slopes(slopes, G, R):
        hi = bf(slopes)
        lo = bf(slopes - hi.astype(F32))
        rows = jnp.pad(jnp.stack([hi, hi, lo, lo], axis=-1), ((0, 0), (0, _QK_FEAT - 4)))
        eff = hi.astype(F32) + lo.astype(F32)
        return rows.reshape(G, R, _QK_FEAT), jnp.broadcast_to(eff.reshape(G, R, 1, 1), (G, R, 1, TQ))

    def flash_q(qz, feat_q, G, R):
        qT = bf(qz.reshape(B, S, G, R, HEAD_DIM) * (HEAD_DIM ** -0.5)).transpose(0, 2, 3, 4, 1)
        return jnp.concatenate([qT, jnp.broadcast_to(feat_q[None, :, :, :, None], (B, G, R, _QK_FEAT, S))], axis=3)

    def flash_kv(kv, G):
        kv5 = kv.reshape(B, S, 2, G, HEAD_DIM)
        k = jnp.concatenate([bf(kv5[:, :, 0].transpose(0, 2, 1, 3)),
                             jnp.broadcast_to(feat_k[None, None], (B, G, S, _QK_FEAT))], axis=-1)
        return k, bf(kv5[:, :, 1].transpose(0, 2, 3, 1))

    def tokens_first(oT):
        return oT.transpose(0, 4, 1, 2, 3).reshape(B, S, -1)

    feat_qa, _ = split_slopes(slopes_a, G_A, R_A)
    qT_a = flash_q(z[..., :NSA_Q], feat_qa, G_A, R_A)
    gates_tt = jnp.pad(glog.transpose(0, 1, 3, 4, 2).reshape(B, G_A, 3 * R_A, S), ((0, 0), (0, 0), (0, 16 - 3 * R_A), (0, 0)))
    k_s, vT_s = flash_kv(kv_s, G_A)
    o_s = _flash(qT_a, k_s, vT_s, tq=TQ, tk=256, bits=bits, gates_t=gates_tt, gate_row=1, name="sel_p")
    k_w, vT_w = flash_kv(kv_w, G_A)
    o_w = _flash(qT_a, k_w, vT_w, tq=TQ, tk=256, window=WIN_A, gates_t=gates_tt, gate_row=2, name="win_p")
    x1 = _mm([_heads_last(o_c), tokens_first(o_s), tokens_first(o_w)], bf(w_o_a[0]), tm=512,
             res=(x_prompt, mp0[2]), name="nsa_out_p")
    x1 = _peer(x1, mp0[4], mp0[3], mp0[5], gn(g_norm[0, 1]), *peer_w[0], t_tile=512, ei=8)

    kv_sh = _mm([x1], bf(w_kv_b), tm=512, norm=(gn(g_kv), mpk[1], mpk[0]), name="kv_sh_p")
    q1 = _mm([x1], bf(w_q_b[0]), tm=512, norm=(gn(g_norm[1, 0]), mp1[1], mp1[0]), name="q_b_p")
    feat_qb, slope_tb = split_slopes(slopes_b, G_B, R_B)
    k_b, vT_b = flash_kv(kv_sh, G_B)
    sink_tb = jnp.broadcast_to(sinks[0].reshape(G_B, R_B, 1, 1), (G_B, R_B, 1, TQ))
    o_b = _flash(flash_q(q1, feat_qb, G_B, R_B), k_b, vT_b, tq=TQ, tk=256, window=WIN_B,
                 sink_t=sink_tb, slope_t=slope_tb, name="swa_p")
    x2 = _mm([tokens_first(o_b)], bf(w_o_b[0]), tm=512, res=(x1, mp1[2]), name="swa_out_p")
    x2 = _peer(x2, mp1[4], mp1[3], mp1[5], gn(g_norm[1, 1]), *peer_w[1], t_tile=512, ei=8)
    y_prompt = _rmsnorm(x2, gn(g_final), tm=512)

    xs = x_sample.reshape(1, Bd, D)
    zs = _mm([xs], w_in, tm=Bd, norm=(gn(g_norm[0, 0]), ms0[1], ms0[0]), name="nsa_in_s")[0]
    kvs_new = [zs[:, NSA_Q + i * NSA_KV:NSA_Q + (i + 1) * NSA_KV] for i in range(3)]
    q16 = bf(zs[:, :NSA_Q] * (HEAD_DIM ** -0.5)).reshape(Bd, H_A, HEAD_DIM)
    gates16 = jnp.pad(zs[:, NSA_Q + 3 * NSA_KV:NSA_IN].reshape(Bd, 3, H_A).transpose(0, 2, 1), ((0, 0), (0, 0), (0, 1)))
    slope16_a = slopes_a.reshape(H_A, 1)

    cmpT = cache_cmp_kv[0].transpose(0, 2, 3, 4, 1)
    n_phys = cmpT.shape[0]
    kvc_new = jnp.pad(kvs_new[0][:, None, :], ((0, 0), (0, SUBLANES - 1), (0, 0)))
    kvcT_s = _cmp_sample(cmpT.reshape(n_phys, 2, 2, 2 * HEAD_DIM, PAGE_SIZE), page_table, kvc_new, w1p, b1, w2T)
    nb_s = kvcT_s.shape[-1] - 1
    ns_pad = 2 * LANES
    ov_s = _overlap_matrix(kvcT_s.shape[-1], ns_pad, nb_s + 1, n_past_blocks + 1)
    o_cs, idx_tile = _dec_cmp(q16, kvcT_s, slope16_a, gates16, ov_s, qpos=past_len, n_past_blocks=n_past_blocks)
    idx_flat = idx_tile[:, :G_A, :N_SEL].reshape(-1)

    selT = cache_sel_kv[0].transpose(0, 2, 3, 4, 1)
    kvs_new5 = kvs_new[1].reshape(Bd, 2, G_A, 1, HEAD_DIM)
    o_ss = _dec_sel(idx_flat, page_table, selT, q16, kvs_new5, slope16_a, gates16,
                    qpos=past_len, n_past_blocks=n_past_blocks)

    winT = state_win_kv[0].transpose(0, 2, 3, 4, 1)
    o_ws = _dec_slab(q16, winT, kvs_new[2].reshape(Bd, 2, G_A, HEAD_DIM), slope16_a, window=WIN_A, qpos=past_len,
                     gates16=gates16, gate_col=2, name="dec_win")
    flat = lambda o: o.reshape(1, Bd, H_A * HEAD_DIM)
    xs1 = _mm([flat(o_cs), flat(o_ss), flat(o_ws)], bf(w_o_a[0]), tm=Bd, res=(xs, ms0[2]), name="nsa_out_s")
    xs1 = _peer(xs1, ms0[4], ms0[3], ms0[5], gn(g_norm[0, 1]), *peer_w[0], t_tile=Bd, ei=8)

    kv_sh_s = _mm([xs1], bf(w_kv_b), tm=Bd, norm=(gn(g_kv), msk[1], msk[0]), name="kv_sh_s")[0]
    q1s = _mm([xs1], bf(w_q_b[0]), tm=Bd, norm=(gn(g_norm[1, 0]), ms1[1], ms1[0]), name="q_b_s")[0]
    q16b = bf(q1s * (HEAD_DIM ** -0.5)).reshape(Bd, H_B, HEAD_DIM)
    shT = state_shared_kv.transpose(0, 2, 3, 4, 1)
    o_bs = _dec_slab(q16b, shT, kv_sh_s.reshape(Bd, 2, G_B, HEAD_DIM), slopes_b.reshape(H_B, 1), window=WIN_B,
                     qpos=past_len, sink16=sinks[0].reshape(H_B, 1), name="dec_swa")
    xs2 = _mm([flat(o_bs)], bf(w_o_b[0]), tm=Bd, res=(xs1, ms1[2]), name="swa_out_s")
    xs2 = _peer(xs2, ms1[4], ms1[3], ms1[5], gn(g_norm[1, 1]), *peer_w[1], t_tile=Bd, ei=8)
    y_sample = _rmsnorm(xs2, gn(g_final), tm=Bd).reshape(Bd, 1, D)

    as_kv = lambda a, n, G: a.reshape(n, -1, 2, G, HEAD_DIM)
    nw = min(WIN_A, S)
    nwb = min(WIN_B, S)
    new_cmp_prompt = as_kv(kv_c, B, G_A)[None]
    new_sel_prompt = as_kv(kv_s, B, G_A)[None]
    new_win_prompt = as_kv(kv_w, B, G_A)[None, :, S - nw:]
    new_cmp_sample = as_kv(kvs_new[0], Bd, G_A)[None]
    new_sel_sample = as_kv(kvs_new[1], Bd, G_A)[None]
    new_win_sample = jnp.concatenate([state_win_kv[0], as_kv(kvs_new[2], Bd, G_A)], axis=1)[None, :, 1:]
    new_shared_prompt = as_kv(kv_sh, B, G_B)[:, S - nwb:]
    new_shared_sample = jnp.concatenate([state_shared_kv, as_kv(kv_sh_s, Bd, G_B)], axis=1)[:, 1:]
    return (y_prompt, y_sample, new_cmp_prompt, new_cmp_sample, new_sel_prompt, new_sel_sample,
            new_win_prompt, new_win_sample, new_shared_prompt, new_shared_sample)
```

```python
import functools

import jax
import jax.numpy as jnp
from jax import lax
from jax.experimental import pallas as pl
from jax.experimental.pallas import tpu as pltpu

F32 = jnp.float32
BF16 = jnp.bfloat16

D_MODEL = 1024
HEAD_DIM = 64
PAGE_SIZE = 128
H_A = 16
G_A = 4
R_A = H_A // G_A
CMP_LEN = 32
CMP_STRIDE = 16
CMP_HID = 128
SEL_BLOCK = 64
N_SEL = 8
WIN_A = 512
NSA_Q = H_A * HEAD_DIM
NSA_KV = 2 * G_A * HEAD_DIM
NSA_IN = NSA_Q + 3 * NSA_KV + 3 * H_A
H_B = 16
G_B = 2
R_B = H_B // G_B
WIN_B = 128
PEER_HEADS = 8
PEER_QDIM = 256
N_KEYS = 128
PEER_TOPK = 16
RMS_EPS = 1e-6
NEG = -1e30
FORCE = 1e3

LANES = 128
SUBLANES = 8
VMEM_LIMIT = 56 * 1024 * 1024


def _cparams(sem):
    return pltpu.CompilerParams(dimension_semantics=sem, vmem_limit_bytes=VMEM_LIMIT)


_GELU_C0 = 0.7978845608028654
_GELU_C1 = 0.7978845608028654 * 0.044715


def _gelu(x):
    return x * (0.5 * (1.0 + jnp.tanh(0.7978845608028654 * (x + 0.044715 * (x * x * x)))))


def _normmod(x, g, sc, sh):
    r = lax.rsqrt(jnp.mean(x * x, axis=-1, keepdims=True) + RMS_EPS)
    return (x * r) * g * (1.0 + sc) + sh


N_TOPV = PEER_TOPK + 1
_PEER_JB = 32
_PAIRS = [(p, q) for p in range(N_TOPV) for q in range(N_TOPV) if (p + 1) * (q + 1) <= N_TOPV]


def _peer_prep_kernel(x_ref, sc_ref, sh_ref, g_ref, wpqT_ref, keys_ref,
                      hT_ref, s1_ref, b_ref, c_ref, a_ref, qT_ref, topv_ref):
    T = x_ref.shape[1]
    h = _normmod(x_ref[0], g_ref[...], sc_ref[0], sh_ref[0])
    hT = h.T.astype(BF16)
    hT_ref[...] = hT
    qT_ref[...] = jnp.dot(wpqT_ref[...], hT, preferred_element_type=F32).astype(BF16)

    halves = (c_ref, s1_ref)
    for hc in range(2 * PEER_HEADS):
        q = qT_ref[hc * N_KEYS:(hc + 1) * N_KEYS, :]
        halves[hc % 2][hc // 2] = jnp.dot(keys_ref[hc], q, preferred_element_type=F32)

    topv_ref[:, 0] = jnp.full((2, PEER_HEADS, T), jnp.inf, F32)

    def topk_round(r, carry):
        for hc in range(2 * PEER_HEADS):
            c, hh = hc % 2, hc // 2
            s = halves[c][hh]
            below = jnp.where(s < topv_ref[c, r, hh:hh + 1, :], s, -jnp.inf)
            topv_ref[c, r + 1, hh:hh + 1, :] = jnp.max(below, axis=0, keepdims=True)
        return carry

    lax.fori_loop(0, N_TOPV, topk_round, 0)

    for tc in range(T // LANES):
        sl = slice(tc * LANES, (tc + 1) * LANES)
        av = [topv_ref[0, p + 1, :, sl] for p in range(N_TOPV)]
        bv = [topv_ref[1, q + 1, :, sl] for q in range(N_TOPV)]
        cands = [av[p] + bv[q] for (p, q) in _PAIRS]
        top = av[0] + bv[0]
        work = list(cands)
        kth = None
        prev = None
        for r in range(N_TOPV):
            m = work[0]
            for w in work[1:]:
                m = jnp.maximum(m, w)
            prev, kth = kth, m
            work = [jnp.where(w >= m, -jnp.inf, w) for w in work]
        tau = 0.5 * (prev + kth)
        z = jnp.zeros_like(top)
        for cnd in cands:
            z = z + jnp.where(cnd >= tau, jnp.exp(cnd - top), 0.0)
        topv_ref[0, 0, :, sl] = tau
        topv_ref[1, 0, :, sl] = 1.0 / z

    for hh in range(PEER_HEADS):
        a1 = topv_ref[0, 1, hh:hh + 1, :]
        b1 = topv_ref[1, 1, hh:hh + 1, :]
        tau = topv_ref[0, 0, hh:hh + 1, :]
        rz = topv_ref[1, 0, hh:hh + 1, :]
        s0 = c_ref[hh]
        c_ref[hh] = tau - s0
        a_ref[hh] = jnp.exp(s0 - a1) * (0.5 * rz)
        b_ref[hh] = jnp.exp(s1_ref[hh] - b1)


def _peer_dense_kernel(hT_ref, s1_ref, b_ref, c_ref, a_ref, u_ref, v_ref, x_ref, ga_ref,
                       o_ref, acc_ref, sa_ref, sb_ref, wa_ref, wb_ref, *, n_i, mw):
    i = pl.program_id(2)
    T = hT_ref.shape[1]
    ei = u_ref.shape[0] // N_KEYS
    s_bufs = (sa_ref, sb_ref)
    w_bufs = (wa_ref, wb_ref)

    tw = min(T, 2 * LANES)
    n_tw = T // tw

    blk = ei * N_KEYS

    def stage_a(dst):
        def piece(k, m):
            s_bufs[dst][m * mw:(m + 1) * mw, k * tw:(k + 1) * tw] = jnp.dot(
                u_ref[m * mw:(m + 1) * mw, :], hT_ref[:, k * tw:(k + 1) * tw], preferred_element_type=F32)
        return [functools.partial(piece, k, m) for k in range(n_tw) for m in range(blk // mw)]

    def stage_b(src):
        s_ref, w_ref = s_bufs[src], w_bufs[src]

        def block(ii, jb):
            jr = slice(jb * _PEER_JB, (jb + 1) * _PEER_JB)
            rows = slice(ii * N_KEYS + jb * _PEER_JB, ii * N_KEYS + (jb + 1) * _PEER_JB)
            w = jnp.zeros((_PEER_JB, T), F32)
            for hh in range(PEER_HEADS):
                cth = c_ref[hh, ii:ii + 1, :]
                ath = a_ref[hh, ii:ii + 1, :]
                w = w + jnp.where(s1_ref[hh, jr, :] >= cth, b_ref[hh, jr, :], 0.0) * ath
            s = s_ref[rows, :]
            g2 = s * (1.0 + jnp.tanh(s * (_GELU_C0 + _GELU_C1 * (s * s))))
            w_ref[rows, :] = (w * g2).astype(BF16)
        return [functools.partial(block, ii, jb) for ii in range(ei) for jb in range(N_KEYS // _PEER_JB)]

    def stage_c(src):
        cw = min(mw, acc_ref.shape[1])

        def piece(k, m):
            acc_ref[k * tw:(k + 1) * tw, m * cw:(m + 1) * cw] += lax.dot_general(
                w_bufs[src][:, k * tw:(k + 1) * tw], v_ref[:, m * cw:(m + 1) * cw], (((0,), (0,)), ((), ())),
                preferred_element_type=F32)
        return [functools.partial(piece, k, m) for k in range(n_tw) for m in range(acc_ref.shape[1] // cw)]

    def run_interleaved(mxu_pieces, vpu_blocks):
        n = len(mxu_pieces)
        per = -(-len(vpu_blocks) // n)
        for k, piece in enumerate(mxu_pieces):
            piece()
            for blk_fn in vpu_blocks[k * per:(k + 1) * per]:
                blk_fn()

    @pl.when(i == 0)
    def _():
        acc_ref[...] = jnp.zeros_like(acc_ref)
        wa_ref[...] = jnp.zeros_like(wa_ref)
        wb_ref[...] = jnp.zeros_like(wb_ref)
        run_interleaved(stage_a(0), [])

    steady = (i >= 1) & (i <= n_i)
    for par in range(2):
        @pl.when(steady & (i % 2 == par))
        def _(par=par):
            run_interleaved(stage_a(par) + stage_c(par), stage_b(1 - par))

    @pl.when(i == n_i + 1)
    def _():
        run_interleaved(stage_c((n_i + 1) % 2), [])
        o_ref[0] = x_ref[0] + ga_ref[0] * acc_ref[...]


def _mod_spec(per_row, t):
    if per_row:
        return pl.BlockSpec((1, t, D_MODEL), lambda b, j, *_: (0, j, 0))
    return pl.BlockSpec((1, 1, D_MODEL), lambda b, j, *_: (b, 0, 0))


def _peer(x, sc, sh, ga, g, wpqT, keys16, u_bf, v_bf, *, t_tile, ei, mw=8 * LANES):
    B, S, D = x.shape
    per_row = sc.shape[1] != 1
    nt = S // t_tile
    ntot = B * S
    n_i = N_KEYS // ei
    col = lambda b, j, *_: (0, b * nt + j)
    col3 = lambda b, j, *_: (0, 0, b * nt + j)
    row3 = lambda b, j, *_: (b, j, 0)
    f = jax.ShapeDtypeStruct
    hT, s1, bb, cc, aa = pl.pallas_call(
        _peer_prep_kernel,
        grid=(B, nt),
        in_specs=[
            pl.BlockSpec((1, t_tile, D), row3),
            _mod_spec(per_row, t_tile), _mod_spec(per_row, t_tile),
            pl.BlockSpec((1, D), lambda b, j: (0, 0)),
            pl.BlockSpec((PEER_HEADS * PEER_QDIM, D), lambda b, j: (0, 0)),
            pl.BlockSpec((2 * PEER_HEADS, N_KEYS, N_KEYS), lambda b, j: (0, 0, 0)),
        ],
        out_specs=[
            pl.BlockSpec((D, t_tile), col),
            pl.BlockSpec((PEER_HEADS, N_KEYS, t_tile), col3),
            pl.BlockSpec((PEER_HEADS, N_KEYS, t_tile), col3),
            pl.BlockSpec((PEER_HEADS, N_KEYS, t_tile), col3),
            pl.BlockSpec((PEER_HEADS, N_KEYS, t_tile), col3),
        ],
        out_shape=[f((D, ntot), BF16)] + [f((PEER_HEADS, N_KEYS, ntot), F32)] * 4,
        scratch_shapes=[pltpu.VMEM((PEER_HEADS * PEER_QDIM, t_tile), BF16),
                        pltpu.VMEM((2, N_TOPV + 1, PEER_HEADS, t_tile), F32)],
        compiler_params=_cparams(("parallel", "parallel")),
        name="peer_prep",
    )(x, sc, sh, g, wpqT, keys16)

    blk = ei * N_KEYS
    blk_a = lambda i: jnp.minimum(i, n_i - 1)
    blk_b = lambda i: jnp.clip(i - 1, 0, n_i - 1)
    blk_c = lambda i: jnp.clip(i - 2, 0, n_i - 1)
    return pl.pallas_call(
        functools.partial(_peer_dense_kernel, n_i=n_i, mw=mw),
        grid=(B, nt, n_i + 2),
        in_specs=[
            pl.BlockSpec((D, t_tile), col),
            pl.BlockSpec((PEER_HEADS, N_KEYS, t_tile), col3),
            pl.BlockSpec((PEER_HEADS, N_KEYS, t_tile), col3),
            pl.BlockSpec((PEER_HEADS, ei, t_tile), lambda b, j, i: (0, blk_b(i), b * nt + j)),
            pl.BlockSpec((PEER_HEADS, ei, t_tile), lambda b, j, i: (0, blk_b(i), b * nt + j)),
            pl.BlockSpec((blk, D), lambda b, j, i: (blk_a(i), 0)),
            pl.BlockSpec((blk, D), lambda b, j, i: (blk_c(i), 0)),
            pl.BlockSpec((1, t_tile, D), row3),
            _mod_spec(per_row, t_tile),
        ],
        out_specs=pl.BlockSpec((1, t_tile, D), row3),
        out_shape=f((B, S, D), F32),
        scratch_shapes=[pltpu.VMEM((t_tile, D), F32),
                        pltpu.VMEM((blk, t_tile), F32), pltpu.VMEM((blk, t_tile), F32),
                        pltpu.VMEM((blk, t_tile), BF16), pltpu.VMEM((blk, t_tile), BF16)],
        compiler_params=_cparams(("parallel", "parallel", "arbitrary")),
        name="peer_dense",
    )(hT, s1, bb, cc, aa, u_bf, v_bf, x, ga)


def _mm_kernel(*refs, n_in, prologue, has_bias, has_res):
    it = iter(refs)
    a_refs = [next(it) for _ in range(n_in)]
    if prologue == "normmod":
        g_ref, sc_ref, sh_ref = next(it), next(it), next(it)
    w_ref = next(it)
    b_ref = next(it) if has_bias else None
    if has_res:
        x_ref, ga_ref = next(it), next(it)
    o_ref, h_ref = next(it), next(it)

    @pl.when(pl.program_id(2) == 0)
    def _():
        a = a_refs[0][0]
        for r in a_refs[1:]:
            a = a + r[0]
        if prologue == "normmod":
            a = _normmod(a, g_ref[...], sc_ref[0], sh_ref[0])
        elif prologue == "silu":
            a = a / (1.0 + jnp.exp(-a))
        h_ref[...] = a.astype(BF16)

    y = jnp.dot(h_ref[...], w_ref[...], preferred_element_type=F32)
    if has_bias:
        y = y + b_ref[...]
    if has_res:
        y = x_ref[0] + ga_ref[0] * y
    o_ref[0] = y.astype(o_ref.dtype)


def _mm(a_list, w, *, tm, tn=None, norm=None, prologue=None, bias=None, res=None, name="mm"):
    B, S, K = a_list[0].shape
    N = w.shape[1]
    tn = N if tn is None else tn
    grid = (B, S // tm, N // tn)
    ins, specs = [], []
    for a in a_list:
        ins.append(a)
        specs.append(pl.BlockSpec((1, tm, K), lambda b, j, n: (b, j, 0)))

    def mod_spec(m, width, tiled_n):
        if m.shape[1] != 1:
            return pl.BlockSpec((1, tm, width), (lambda b, j, n: (0, j, n)) if tiled_n else (lambda b, j, n: (0, j, 0)))
        return pl.BlockSpec((1, 1, width), (lambda b, j, n: (b, 0, n)) if tiled_n else (lambda b, j, n: (b, 0, 0)))

    if norm is not None:
        g, sc, sh = norm
        prologue = "normmod"
        ins += [g, sc, sh]
        specs += [pl.BlockSpec((1, K), lambda b, j, n: (0, 0)), mod_spec(sc, K, False), mod_spec(sh, K, False)]
    ins.append(w)
    specs.append(pl.BlockSpec((K, tn), lambda b, j, n: (0, n)))
    if bias is not None:
        ins.append(bias)
        specs.append(pl.BlockSpec((1, tn), lambda b, j, n: (0, n)))
    if res is not None:
        x, ga = res
        ins += [x, ga]
        specs += [pl.BlockSpec((1, tm, tn), lambda b, j, n: (b, j, n)), mod_spec(ga, tn, True)]
    return pl.pallas_call(
        functools.partial(_mm_kernel, n_in=len(a_list), prologue=prologue,
                          has_bias=bias is not None, has_res=res is not None),
        grid=grid, in_specs=specs,
        out_specs=pl.BlockSpec((1, tm, tn), lambda b, j, n: (b, j, n)),
        out_shape=jax.ShapeDtypeStruct((B, S, N), F32),
        scratch_shapes=[pltpu.VMEM((tm, K), BF16)],
        compiler_params=_cparams(("parallel", "parallel", "arbitrary")),
        name=name,
    )(*ins)


def _rmsnorm_kernel(x_ref, g_ref, o_ref):
    x = x_ref[0]
    r = lax.rsqrt(jnp.mean(x * x, axis=-1, keepdims=True) + RMS_EPS)
    o_ref[0] = (x * r) * g_ref[...]


def _rmsnorm(x, g, *, tm):
    B, S, D = x.shape
    return pl.pallas_call(
        _rmsnorm_kernel, grid=(B, S // tm),
        in_specs=[pl.BlockSpec((1, tm, D), lambda b, j: (b, j, 0)), pl.BlockSpec((1, D), lambda b, j: (0, 0))],
        out_specs=pl.BlockSpec((1, tm, D), lambda b, j: (b, j, 0)),
        out_shape=jax.ShapeDtypeStruct((B, S, D), F32),
        compiler_params=_cparams(("parallel", "parallel")), name="final_norm",
    )(x, g)


def _chunk_proj_steps(x_ref, w1_ref, n_chunks, p_ref):
    n_steps = CMP_STRIDE // 2
    state = {}

    def step(s2):
        l0 = x_ref[pl.ds(2 * s2, n_chunks, stride=CMP_STRIDE), :]
        l1 = x_ref[pl.ds(2 * s2 + 1, n_chunks, stride=CMP_STRIDE), :]
        lhs = jnp.concatenate([l0, l1], axis=-1).astype(BF16)
        d = jnp.dot(lhs, w1_ref[0, s2], preferred_element_type=F32)
        state["acc"] = d if s2 == 0 else state["acc"] + d
        if s2 == n_steps - 1:
            p_ref[0:n_chunks, :] = state.pop("acc")
    return [functools.partial(step, s2) for s2 in range(n_steps)]


def _chunk_proj(x_ref, w1_ref, n_chunks, p_ref):
    for step in _chunk_proj_steps(x_ref, w1_ref, n_chunks, p_ref):
        step()


def _cmp_finish(p_ref, nb, b1_ref, w2T_ref, o_ref, g_base=0):
    for gp in range(2):
        c0 = gp * 2 * CMP_HID
        acc = p_ref[0:nb, c0:c0 + CMP_HID] + p_ref[1:nb + 1, c0 + CMP_HID:c0 + 2 * CMP_HID]
        hid = _gelu(acc + b1_ref[0]).astype(BF16)
        kT = lax.dot_general(w2T_ref[0], hid, (((1,), (1,)), ((), ())), preferred_element_type=F32)
        o_ref[0, 0, g_base + gp] = kT.astype(o_ref.dtype)


def _cmp_prompt_kernel(x_ref, w1_ref, b1_ref, w2T_ref, o_ref, p_ref, *, n_chunks, nbp):
    p_ref[n_chunks:, :] = jnp.zeros((p_ref.shape[0] - n_chunks, p_ref.shape[1]), F32)
    _chunk_proj(x_ref.at[0], w1_ref, n_chunks, p_ref)
    _cmp_finish(p_ref, nbp, b1_ref, w2T_ref, o_ref)


def _cmp_sample_kernel(pt_ref, *refs, n_pages, n_chunks):
    page_refs = refs[:n_pages]
    xn_ref, w1_ref, b1_ref, w2T_ref, o_ref, xs_ref, p_ref = refs[n_pages:]

    def transposes(gp):
        per = max(1, n_pages * 2 // CMP_STRIDE)

        def some(p0):
            for p in range(p0, min(p0 + per, n_pages)):
                xs_ref[gp, p * PAGE_SIZE:(p + 1) * PAGE_SIZE, :] = page_refs[p][0, 0, gp].T
        return [functools.partial(some, p0) for p0 in range(0, n_pages, per)]

    def finish(gp):
        pg = p_ref.at[gp]
        new = jnp.dot(xn_ref[0, :, gp * LANES:(gp + 1) * LANES].astype(BF16), w1_ref[0, 0, 0:LANES, :],
                      preferred_element_type=F32)
        pg[n_chunks:n_chunks + SUBLANES, :] = new
        _cmp_finish(pg, n_chunks, b1_ref, w2T_ref, o_ref, g_base=2 * gp)

    for t in transposes(0):
        t()
    t1 = transposes(1)
    p0 = _chunk_proj_steps(xs_ref.at[0], w1_ref, n_chunks, p_ref.at[0])
    for k in range(max(len(t1), len(p0))):
        if k < len(t1):
            t1[k]()
        if k < len(p0):
            p0[k]()
    finish(0)
    _chunk_proj(xs_ref.at[1], w1_ref, n_chunks, p_ref.at[1])
    finish(1)


def _cmp_weights(w1, w2):
    w1r = w1.reshape(2, 2, CMP_STRIDE, HEAD_DIM, CMP_HID).transpose(0, 2, 3, 1, 4)
    w1r = w1r.reshape(2, CMP_STRIDE, HEAD_DIM, 2 * CMP_HID)
    eye = jnp.eye(2, dtype=F32)
    wp = jnp.einsum("ab,ksdn->ksadbn", eye, w1r).reshape(2, CMP_STRIDE // 2, 4 * HEAD_DIM, 4 * CMP_HID)
    return wp.astype(BF16), w2.transpose(0, 2, 1).astype(BF16)


def _cmp_prompt(kv_c, w1p, b1, w2T):
    B, S, _ = kv_c.shape
    n_chunks = S // CMP_STRIDE
    nbp = n_chunks
    return pl.pallas_call(
        functools.partial(_cmp_prompt_kernel, n_chunks=n_chunks, nbp=nbp),
        grid=(B, 2, 2),
        in_specs=[
            pl.BlockSpec((1, S, LANES), lambda b, k, gp: (b, 0, k * 2 + gp)),
            pl.BlockSpec((1, CMP_STRIDE // 2, 2 * LANES, 4 * CMP_HID), lambda b, k, gp: (k, 0, 0, 0)),
            pl.BlockSpec((1, 1, CMP_HID), lambda b, k, gp: (k, 0, 0)),
            pl.BlockSpec((1, HEAD_DIM, CMP_HID), lambda b, k, gp: (k, 0, 0)),
        ],
        out_specs=pl.BlockSpec((1, 1, 2, HEAD_DIM, nbp), lambda b, k, gp: (b, k, gp, 0, 0)),
        out_shape=jax.ShapeDtypeStruct((B, 2, G_A, HEAD_DIM, nbp), BF16),
        scratch_shapes=[pltpu.VMEM((n_chunks + SUBLANES, 4 * CMP_HID), F32)],
        compiler_params=_cparams(("parallel", "parallel", "parallel")), name="cmp_prompt",
    )(kv_c, w1p, b1, w2T)


def _cmp_sample(cacheT, page_table, kv_c_new, w1p, b1, w2T):
    Bd, n_pages = page_table.shape
    n_chunks = n_pages * PAGE_SIZE // CMP_STRIDE

    def page_spec(p):
        return pl.BlockSpec((1, 1, 2, LANES, PAGE_SIZE), lambda b, k, pt: (pt[b, p], k, 0, 0, 0))

    return pl.pallas_call(
        functools.partial(_cmp_sample_kernel, n_pages=n_pages, n_chunks=n_chunks),
        grid_spec=pltpu.PrefetchScalarGridSpec(
            num_scalar_prefetch=1, grid=(Bd, 2),
            in_specs=[page_spec(p) for p in range(n_pages)] + [
                pl.BlockSpec((1, SUBLANES, 2 * LANES), lambda b, k, pt: (b, 0, k)),
                pl.BlockSpec((1, CMP_STRIDE // 2, 2 * LANES, 4 * CMP_HID), lambda b, k, pt: (k, 0, 0, 0)),
                pl.BlockSpec((1, 1, CMP_HID), lambda b, k, pt: (k, 0, 0)),
                pl.BlockSpec((1, HEAD_DIM, CMP_HID), lambda b, k, pt: (k, 0, 0)),
            ],
            out_specs=pl.BlockSpec((1, 1, G_A, HEAD_DIM, n_chunks), lambda b, k, pt: (b, k, 0, 0, 0)),
            scratch_shapes=[pltpu.VMEM((2, n_pages * PAGE_SIZE, LANES), F32),
                            pltpu.VMEM((2, n_chunks + SUBLANES, 4 * CMP_HID), F32)],
        ),
        out_shape=jax.ShapeDtypeStruct((Bd, 2, G_A, HEAD_DIM, n_chunks), BF16),
        compiler_params=_cparams(("parallel", "parallel")), name="cmp_sample",
    )(page_table, *([cacheT] * n_pages), kv_c_new, w1p, b1, w2T)


_LOWEST = -3.0e38


def _select_rounds(score, lane):
    picks = []
    for _ in range(N_SEL):
        m = jnp.max(score, axis=-1, keepdims=True)
        idx = jnp.min(jnp.where(score == m, lane, 1 << 20), axis=-1, keepdims=True)
        picks.append(idx)
        score = jnp.where(lane == idx, _LOWEST, score)
    return picks


def _overlap_matrix(nb_pad, ns_pad, nb, ns):
    cs = jnp.arange(nb_pad) * CMP_STRIDE
    ss = jnp.arange(ns_pad) * SEL_BLOCK
    ov = jnp.maximum(jnp.minimum(cs[:, None] + CMP_LEN, ss[None, :] + SEL_BLOCK)
                     - jnp.maximum(cs[:, None], ss[None, :]), 0).astype(F32) / CMP_LEN
    ov = jnp.where((jnp.arange(nb_pad)[:, None] < nb) & (jnp.arange(ns_pad)[None, :] < ns), ov, 0.0)
    return ov.astype(BF16)


def _row_info(rows, tq, q0):
    row = lax.broadcasted_iota(jnp.int32, (rows, 1), 0)
    return q0 + (row & (tq - 1))


def _cmp_attend_kernel(q_ref, kT_ref, vT_ref, slope_ref, gate_ref, ov_ref, o_ref, bits_ref, *, tq, nb, ns):
    R = q_ref.shape[2]
    rows = R * tq
    q0 = pl.program_id(2) * tq
    q = q_ref[0, 0].reshape(rows, HEAD_DIM)
    qpos = _row_info(rows, tq, q0)
    slope = slope_ref[0]
    nbp = kT_ref.shape[-1]
    blk_end = lax.broadcasted_iota(jnp.int32, (1, nbp), 1) * CMP_STRIDE + (CMP_LEN - 1)
    s = jnp.dot(q, kT_ref[0, 0, 0], preferred_element_type=F32)
    s = s + slope * (blk_end - q0).astype(F32)
    valid = (blk_end <= qpos) & (blk_end < nb * CMP_STRIDE + CMP_LEN - 1)
    s = jnp.where(valid, s, NEG)
    m = jnp.max(s, axis=-1, keepdims=True)
    e = jnp.exp(s - m)
    p = e * (jnp.where(qpos >= CMP_LEN - 1, 1.0, 0.0) / jnp.sum(e, axis=-1, keepdims=True))
    pb = p.astype(BF16)
    o = lax.dot_general(pb, vT_ref[0, 0, 0], (((1,), (1,)), ((), ())), preferred_element_type=F32)
    imp_r = jnp.dot(pb, ov_ref[...], preferred_element_type=F32)
    g = gate_ref[0, 0]
    imp = None
    for r in range(R):
        gate = 1.0 / (1.0 + jnp.exp(-g[:, r:r + 1]))
        o_ref[0, 0, r] = o[r * tq:(r + 1) * tq] * gate
        part = imp_r[r * tq:(r + 1) * tq]
        imp = part if imp is None else imp + part
    tpos = qpos[0:tq]
    lane = lax.broadcasted_iota(jnp.int32, (1, imp.shape[1]), 1)
    cur = tpos >> 6
    forced = (lane == 0) | (lane == cur) | (lane == cur - 1)
    score = jnp.where(lane * SEL_BLOCK <= tpos, imp + jnp.where(forced, FORCE, 0.0), NEG)
    score = jnp.where(lane < ns, score, _LOWEST)
    bits = jnp.zeros((tq, 1), jnp.int32)
    for idx in _select_rounds(score, lane):
        bits = bits | (1 << idx)
    bits_ref[0, 0] = jnp.transpose(jnp.broadcast_to(bits, (tq, LANES)))[0:SUBLANES, :]


def _cmp_attend_prompt(q_t, kvcT, slope_rows, gates_t, ov, *, tq, nb, ns):
    B, G, R, S, _ = q_t.shape
    nbp = kvcT.shape[-1]
    return pl.pallas_call(
        functools.partial(_cmp_attend_kernel, tq=tq, nb=nb, ns=ns),
        grid=(B, G, S // tq),
        in_specs=[
            pl.BlockSpec((1, 1, R, tq, HEAD_DIM), lambda b, g, i: (b, g, 0, i, 0)),
            pl.BlockSpec((1, 1, 1, HEAD_DIM, nbp), lambda b, g, i: (b, 0, g, 0, 0)),
            pl.BlockSpec((1, 1, 1, HEAD_DIM, nbp), lambda b, g, i: (b, 1, g, 0, 0)),
            pl.BlockSpec((1, R * tq, 1), lambda b, g, i: (g, 0, 0)),
            pl.BlockSpec((1, 1, tq, 16), lambda b, g, i: (b, g, i, 0)),
            pl.BlockSpec(ov.shape, lambda b, g, i: (0, 0)),
        ],
        out_specs=[
            pl.BlockSpec((1, 1, R, tq, HEAD_DIM), lambda b, g, i: (b, g, 0, i, 0)),
            pl.BlockSpec((1, 1, SUBLANES, tq), lambda b, g, i: (b, g, 0, i)),
        ],
        out_shape=[jax.ShapeDtypeStruct((B, G, R, S, HEAD_DIM), F32),
                   jax.ShapeDtypeStruct((B, G, SUBLANES, S), jnp.int32)],
        compiler_params=_cparams(("parallel", "parallel", "parallel")), name="cmp_attend_prompt",
    )(q_t, kvcT, kvcT, slope_rows, gates_t, ov)


_QK_FEAT = 16


def _flash_kernel(*refs, tq, tk, window, use_bits, gate_row, use_sink):
    it = iter(refs)
    qT_ref, k_ref, vT_ref = next(it), next(it), next(it)
    bits_ref = next(it) if use_bits else None
    gate_ref = next(it) if gate_row is not None else None
    if use_sink:
        sink_ref, slope_ref = next(it), next(it)
    o_ref = next(it)
    m_ref, l_ref, acc_ref = next(it), next(it), next(it)
    R = qT_ref.shape[2]
    qi = pl.program_id(2)
    q0 = qi * tq
    tpos = q0 + lax.broadcasted_iota(jnp.int32, (1, tq), 1)
    if use_bits:
        bits = bits_ref[0, 0, 0:1, :]
    hi = (q0 + tq + tk - 1) // tk
    lo = 0 if window is None else jnp.maximum(q0 - (window - 1), 0) // tk
    m_ref[...] = jnp.full(m_ref.shape, NEG, F32)
    l_ref[...] = jnp.zeros(l_ref.shape, F32)
    acc_ref[...] = jnp.zeros(acc_ref.shape, F32)

    def body(j, carry):
        k0 = pl.multiple_of(j * tk, tk)
        kt = k_ref[0, 0, pl.ds(k0, tk), :]
        vT = vT_ref[0, 0, :, pl.ds(k0, tk)]
        kpos = k0 + lax.broadcasted_iota(jnp.int32, (tk, 1), 0)
        mask = kpos <= tpos
        if window is not None:
            mask = mask & (kpos > tpos - window)
        if use_bits:
            mask = mask & ((bits & (1 << (kpos >> 6))) != 0)
        mbias = jnp.where(mask, 0.0, NEG)
        hs = range(R)
        ss = [jnp.dot(kt, qT_ref[0, 0, r], preferred_element_type=F32) + mbias for r in hs]
        m_olds = [m_ref[r] for r in hs]
        m_news = [jnp.maximum(m_olds[r], jnp.max(ss[r], axis=0, keepdims=True)) for r in hs]
        alphas = [jnp.exp(m_olds[r] - m_news[r]) for r in hs]
        ps = [jnp.exp(ss[r] - m_news[r]) for r in hs]
        sums = [jnp.sum(ps[r], axis=0, keepdims=True) for r in hs]
        pvs = [jnp.dot(vT, ps[r].astype(BF16), preferred_element_type=F32) for r in hs]
        for r in hs:
            l_ref[r] = alphas[r] * l_ref[r] + sums[r]
            acc_ref[r] = alphas[r] * acc_ref[r] + pvs[r]
            m_ref[r] = m_news[r]
        return carry

    lax.fori_loop(lo, hi, body, 0)
    for r in range(R):
        m, l, acc = m_ref[r], l_ref[r], acc_ref[r]
        if use_sink:
            sk = sink_ref[0, r] + slope_ref[0, r] * tpos.astype(F32)
            m_new = jnp.maximum(m, sk)
            alpha = jnp.exp(m - m_new)
            l = alpha * l + jnp.exp(sk - m_new)
            acc = alpha * acc
        scale = 1.0 / l
        if gate_row is not None:
            c = gate_row * R + r
            scale = scale * (1.0 / (1.0 + jnp.exp(-gate_ref[0, 0, c:c + 1, :])))
        o_ref[0, 0, r] = acc * scale


def _flash(qT, k, vT, *, tq, tk, window=None, bits=None, gates_t=None, gate_row=None,
           sink_t=None, slope_t=None, name="flash"):
    B, G, R, KD, S = qT.shape
    ins = [qT, k, vT]
    specs = [
        pl.BlockSpec((1, 1, R, KD, tq), lambda b, g, i: (b, g, 0, 0, i)),
        pl.BlockSpec((1, 1, S, KD), lambda b, g, i: (b, g, 0, 0)),
        pl.BlockSpec((1, 1, HEAD_DIM, S), lambda b, g, i: (b, g, 0, 0)),
    ]
    if bits is not None:
        ins.append(bits)
        specs.append(pl.BlockSpec((1, 1, SUBLANES, tq), lambda b, g, i: (b, g, 0, i)))
    if gates_t is not None:
        ins.append(gates_t)
        specs.append(pl.BlockSpec((1, 1, 16, tq), lambda b, g, i: (b, g, 0, i)))
    if sink_t is not None:
        ins += [sink_t, slope_t]
        specs += [pl.BlockSpec((1, R, 1, tq), lambda b, g, i: (g, 0, 0, 0))] * 2
    return pl.pallas_call(
        functools.partial(_flash_kernel, tq=tq, tk=tk, window=window, use_bits=bits is not None,
                          gate_row=gate_row if gates_t is not None else None, use_sink=sink_t is not None),
        grid=(B, G, S // tq), in_specs=specs,
        out_specs=pl.BlockSpec((1, 1, R, HEAD_DIM, tq), lambda b, g, i: (b, g, 0, 0, i)),
        out_shape=jax.ShapeDtypeStruct((B, G, R, HEAD_DIM, S), F32),
        scratch_shapes=[pltpu.VMEM((R, 1, tq), F32), pltpu.VMEM((R, 1, tq), F32),
                        pltpu.VMEM((R, HEAD_DIM, tq), F32)],
        compiler_params=_cparams(("parallel", "parallel", "parallel")), name=name,
    )(*ins)


def _softmax_parts(s, valid, extra_s, extra_valid, sink):
    s = jnp.where(valid, s, NEG)
    m = jnp.max(s, axis=-1, keepdims=True)
    if extra_s is not None:
        extra_s = jnp.where(extra_valid, extra_s, NEG)
        m = jnp.maximum(m, extra_s)
    if sink is not None:
        m = jnp.maximum(m, sink)
    e = jnp.exp(s - m)
    l = jnp.sum(e, axis=-1, keepdims=True)
    e_new = None
    if extra_s is not None:
        e_new = jnp.where(extra_valid, jnp.exp(extra_s - m), 0.0)
        l = l + e_new
    if sink is not None:
        l = l + jnp.exp(sink - m)
    return e, e_new, l


def _new_key_score(q16, k_new):
    return jnp.sum(q16.astype(F32) * k_new.astype(BF16).astype(F32), axis=-1, keepdims=True)


def _dec_cmp_kernel(q_ref, kT_ref, vT_ref, slope_ref, gate_ref, ov_ref, o_ref, idx_ref, *, qpos, n_past_blocks):
    q16 = q_ref[0]
    slope = slope_ref[...]
    nb = kT_ref.shape[-1]
    nsp = ov_ref.shape[1]
    blk_end = lax.broadcasted_iota(jnp.int32, (1, nb), 1) * CMP_STRIDE + (CMP_LEN - 1)
    valid = blk_end <= qpos
    bias = slope * (blk_end - qpos).astype(F32)
    row = lax.broadcasted_iota(jnp.int32, (H_A, 1), 0)
    lane = lax.broadcasted_iota(jnp.int32, (1, nsp), 1)
    cur = qpos // SEL_BLOCK
    forced = (lane == 0) | (lane == cur) | (lane == cur - 1)
    ns = n_past_blocks + 1
    o = jnp.zeros((H_A, HEAD_DIM), F32)
    tile_l = lax.broadcasted_iota(jnp.int32, (SUBLANES, LANES), 1)
    row8 = lax.broadcasted_iota(jnp.int32, (SUBLANES, 1), 0)
    score8 = jnp.full((SUBLANES, nsp), _LOWEST, F32)
    gs = range(G_A)
    ss = [jnp.where(valid, jnp.dot(q16, kT_ref[0, 0, g], preferred_element_type=F32) + bias, NEG) for g in gs]
    ms = [jnp.max(ss[g], axis=-1, keepdims=True) for g in gs]
    es = [jnp.exp(ss[g] - ms[g]) for g in gs]
    ls = [jnp.sum(es[g], axis=-1, keepdims=True) for g in gs]
    pbs = [(es[g] / ls[g]).astype(BF16) for g in gs]
    ogs = [lax.dot_general(pbs[g], vT_ref[0, 0, g], (((1,), (1,)), ((), ())), preferred_element_type=F32)
           for g in gs]
    imps = [jnp.dot(pbs[g], ov_ref[...], preferred_element_type=F32) for g in gs]
    for g in gs:
        mine = (row >> 2) == g
        o = jnp.where(mine, ogs[g], o)
        imp = jnp.sum(jnp.where(mine, imps[g], 0.0), axis=0, keepdims=True)
        score = jnp.where(lane * SEL_BLOCK <= qpos, imp + jnp.where(forced, FORCE, 0.0), NEG)
        score = jnp.where(lane < ns, score, _LOWEST)
        score8 = jnp.where(row8 == g, score, score8)
    tile = jnp.zeros((SUBLANES, LANES), jnp.int32)
    for r, idx in enumerate(_select_rounds(score8, lane)):
        tile = jnp.where(tile_l == r, idx, tile)
    gate = 1.0 / (1.0 + jnp.exp(-gate_ref[0][:, 0:1]))
    o_ref[0] = o * gate
    idx_ref[0] = tile


def _dec_cmp(q16, kvcT, slope16, gates16, ov, *, qpos, n_past_blocks):
    Bd = q16.shape[0]
    nb = kvcT.shape[-1]
    return pl.pallas_call(
        functools.partial(_dec_cmp_kernel, qpos=qpos, n_past_blocks=n_past_blocks),
        grid=(Bd,),
        in_specs=[
            pl.BlockSpec((1, H_A, HEAD_DIM), lambda b: (b, 0, 0)),
            pl.BlockSpec((1, 1, G_A, HEAD_DIM, nb), lambda b: (b, 0, 0, 0, 0)),
            pl.BlockSpec((1, 1, G_A, HEAD_DIM, nb), lambda b: (b, 1, 0, 0, 0)),
            pl.BlockSpec((H_A, 1), lambda b: (0, 0)),
            pl.BlockSpec((1, H_A, 4), lambda b: (b, 0, 0)),
            pl.BlockSpec(ov.shape, lambda b: (0, 0)),
        ],
        out_specs=[pl.BlockSpec((1, H_A, HEAD_DIM), lambda b: (b, 0, 0)),
                   pl.BlockSpec((1, SUBLANES, LANES), lambda b: (b, 0, 0))],
        out_shape=[jax.ShapeDtypeStruct((Bd, H_A, HEAD_DIM), F32),
                   jax.ShapeDtypeStruct((Bd, SUBLANES, LANES), jnp.int32)],
        compiler_params=_cparams(("parallel",)), name="dec_cmp",
    )(q16, kvcT, kvcT, slope16, gates16, ov)


def _dec_sel_kernel(idx_ref, pt_ref, *refs, qpos, n_past_blocks):
    kv_refs = refs[:G_A * N_SEL]
    q_ref, kvn_ref, slope_ref, gate_ref, o_ref = refs[G_A * N_SEL:]
    b = pl.program_id(0)
    q16 = q_ref[0]
    slope = slope_ref[...]
    lane = lax.broadcasted_iota(jnp.int32, (1, PAGE_SIZE), 1)
    row = lax.broadcasted_iota(jnp.int32, (H_A, 1), 0)
    o = jnp.zeros((H_A, HEAD_DIM), F32)
    for g in range(G_A):
        s_l, valid_l, v_l = [], [], []
        any_new = None
        for kk in range(N_SEL):
            idx = idx_ref[(b * G_A + g) * N_SEL + kk]
            is_new = (jnp.zeros((H_A, 1), jnp.int32) + idx) >= n_past_blocks
            any_new = is_new if any_new is None else (any_new | is_new)
            jp = jnp.minimum(idx, n_past_blocks - 1)
            kpos = (jp >> 1) * PAGE_SIZE + lane
            tile = kv_refs[g * N_SEL + kk]
            s = jnp.dot(q16, tile[0, 0, 0].astype(BF16), preferred_element_type=F32)
            s_l.append(s - slope * (qpos - kpos).astype(F32))
            in_block = ((lane >> 6) == (jp & 1)) & ((lane * 0 + idx) < n_past_blocks) & (kpos <= qpos)
            valid_l.append(jnp.broadcast_to(in_block, (H_A, PAGE_SIZE)))
            v_l.append(tile[0, 1, 0].astype(BF16))
        s = jnp.concatenate(s_l, axis=-1)
        valid = jnp.concatenate(valid_l, axis=-1)
        vT = jnp.concatenate(v_l, axis=-1)
        s_new = _new_key_score(q16, kvn_ref[0, 0, g])
        e, e_new, l = _softmax_parts(s, valid, s_new, any_new, None)
        og = lax.dot_general(e.astype(BF16), vT, (((1,), (1,)), ((), ())), preferred_element_type=F32)
        og = (og + e_new * kvn_ref[0, 1, g].astype(BF16).astype(F32)) / l
        o = jnp.where((row >> 2) == g, og, o)
    gate = 1.0 / (1.0 + jnp.exp(-gate_ref[0][:, 1:2]))
    o_ref[0] = o * gate


def _dec_sel(idx_flat, page_table, cacheT, q16, kv_new, slope16, gates16, *, qpos, n_past_blocks):
    Bd = q16.shape[0]

    def tile_spec(g, kk):
        def imap(b, idx, pt):
            jp = jnp.minimum(idx[(b * G_A + g) * N_SEL + kk], n_past_blocks - 1)
            return (pt[b, jp >> 1], 0, g, 0, 0)
        return pl.BlockSpec((1, 2, 1, HEAD_DIM, PAGE_SIZE), imap)

    return pl.pallas_call(
        functools.partial(_dec_sel_kernel, qpos=qpos, n_past_blocks=n_past_blocks),
        grid_spec=pltpu.PrefetchScalarGridSpec(
            num_scalar_prefetch=2, grid=(Bd,),
            in_specs=[tile_spec(g, kk) for g in range(G_A) for kk in range(N_SEL)] + [
                pl.BlockSpec((1, H_A, HEAD_DIM), lambda b, idx, pt: (b, 0, 0)),
                pl.BlockSpec((1, 2, G_A, 1, HEAD_DIM), lambda b, idx, pt: (b, 0, 0, 0, 0)),
                pl.BlockSpec((H_A, 1), lambda b, idx, pt: (0, 0)),
                pl.BlockSpec((1, H_A, 4), lambda b, idx, pt: (b, 0, 0)),
            ],
            out_specs=pl.BlockSpec((1, H_A, HEAD_DIM), lambda b, idx, pt: (b, 0, 0)),
        ),
        out_shape=jax.ShapeDtypeStruct((Bd, H_A, HEAD_DIM), F32),
        compiler_params=_cparams(("parallel",)), name="dec_sel",
    )(idx_flat, page_table, *([cacheT] * (G_A * N_SEL)), q16, kv_new, slope16, gates16)


def _dec_slab_kernel(*refs, n_groups, window, qpos, gate_col, use_sink):
    it = iter(refs)
    q_ref, kT_ref, vT_ref, kn_ref, vn_ref, slope_ref = (next(it) for _ in range(6))
    gate_ref = next(it) if gate_col is not None else None
    sink_ref = next(it) if use_sink else None
    o_ref = next(it)
    q16 = q_ref[0]
    slope = slope_ref[...]
    nbuf = kT_ref.shape[-1]
    rpg = q16.shape[0] // n_groups
    kpos = qpos - nbuf + lax.broadcasted_iota(jnp.int32, (1, nbuf), 1)
    dist = qpos - kpos
    valid = (dist >= 0) & (dist < window) & (kpos >= 0)
    bias = -slope * dist.astype(F32)
    row = lax.broadcasted_iota(jnp.int32, (q16.shape[0], 1), 0)
    sink = sink_ref[...] if use_sink else None
    o = jnp.zeros((q16.shape[0], HEAD_DIM), F32)
    for g in range(n_groups):
        s = jnp.dot(q16, kT_ref[0, 0, g].astype(BF16), preferred_element_type=F32) + bias
        s_new = _new_key_score(q16, kn_ref[0, 0, g:g + 1, :])
        e, e_new, l = _softmax_parts(s, valid, s_new, True, sink)
        og = lax.dot_general(e.astype(BF16), vT_ref[0, 0, g].astype(BF16), (((1,), (1,)), ((), ())),
                             preferred_element_type=F32)
        og = (og + e_new * vn_ref[0, 0, g:g + 1, :].astype(BF16).astype(F32)) / l
        o = jnp.where((row // rpg) == g, og, o)
    if gate_col is not None:
        o = o * (1.0 / (1.0 + jnp.exp(-gate_ref[0][:, gate_col:gate_col + 1])))
    o_ref[0] = o


def _dec_slab(q16, slabT, kv_new, slope16, *, window, qpos, gates16=None, gate_col=None, sink16=None, name):
    Bd, _, G, _, nbuf = slabT.shape
    H = q16.shape[1]
    ins = [q16, slabT, slabT, kv_new, kv_new, slope16]
    specs = [
        pl.BlockSpec((1, H, HEAD_DIM), lambda b: (b, 0, 0)),
        pl.BlockSpec((1, 1, G, HEAD_DIM, nbuf), lambda b: (b, 0, 0, 0, 0)),
        pl.BlockSpec((1, 1, G, HEAD_DIM, nbuf), lambda b: (b, 1, 0, 0, 0)),
        pl.BlockSpec((1, 1, G, HEAD_DIM), lambda b: (b, 0, 0, 0)),
        pl.BlockSpec((1, 1, G, HEAD_DIM), lambda b: (b, 1, 0, 0)),
        pl.BlockSpec((H, 1), lambda b: (0, 0)),
    ]
    if gates16 is not None:
        ins.append(gates16)
        specs.append(pl.BlockSpec((1, H, 4), lambda b: (b, 0, 0)))
    if sink16 is not None:
        ins.append(sink16)
        specs.append(pl.BlockSpec((H, 1), lambda b: (0, 0)))
    return pl.pallas_call(
        functools.partial(_dec_slab_kernel, n_groups=G, window=window, qpos=qpos,
                          gate_col=gate_col if gates16 is not None else None, use_sink=sink16 is not None),
        grid=(Bd,), in_specs=specs,
        out_specs=pl.BlockSpec((1, H, HEAD_DIM), lambda b: (b, 0, 0)),
        out_shape=jax.ShapeDtypeStruct((Bd, H, HEAD_DIM), F32),
        compiler_params=_cparams(("parallel",)), name=name,
    )(*ins)


def _alibi(n_heads):
    return 2.0 ** (-8.0 * jnp.arange(1, n_heads + 1, dtype=F32) / n_heads)


def _heads_first(z, G, R):
    B, S, _ = z.shape
    return z.reshape(B, S, G, R, HEAD_DIM).transpose(0, 2, 3, 1, 4)


def _heads_last(o):
    B, G, R, S, _ = o.shape
    return o.transpose(0, 3, 1, 2, 4).reshape(B, S, G * R * HEAD_DIM)


def kernel(x_prompt, x_sample, c_prompt, c_sample, cache_cmp_kv, cache_sel_kv, state_win_kv, state_shared_kv, page_table, w_ada, b_ada, g_norm, w_in_a, w_o_a, cmp_pos, w_cmp1, b_cmp1, w_cmp2, g_kv, w_ada_kv, b_ada_kv, w_kv_b, w_q_b, w_o_b, sinks, w_pq, peer_keys, peer_u, peer_v, g_final):
    B, S, D = x_prompt.shape
    Bd = x_sample.shape[0]
    n_pages = page_table.shape[1]
    past_len = n_pages * PAGE_SIZE
    n_past_blocks = past_len // SEL_BLOCK
    TQ = 256
    TQ_CMP = 1024
    bf = lambda a: a.astype(BF16)

    c_all = jnp.concatenate([c_prompt, c_sample], axis=0)[None]
    nc = B + Bd

    def ada(w, b):
        return _mm([c_all], bf(w), tm=nc, tn=2048, prologue="silu", bias=b[None], name="ada")[0]

    mods = [ada(w_ada[i], b_ada[i]) for i in range(2)]
    mods_kv = ada(w_ada_kv, b_ada_kv)

    def split_mods(m, n):
        parts = jnp.split(m, n, axis=-1)
        return [p[:B, None, :] for p in parts], [p[None, B:, :] for p in parts]

    mp0, ms0 = split_mods(mods[0], 6)
    mp1, ms1 = split_mods(mods[1], 6)
    mpk, msk = split_mods(mods_kv, 2)

    w_in = bf(jnp.pad(w_in_a[0], ((0, 0), (0, 21 * LANES - NSA_IN))))
    w1p, w2T = _cmp_weights(w_cmp1[0], w_cmp2[0])
    pos_rows = jnp.pad(cmp_pos[0].reshape(2, 1, CMP_LEN * HEAD_DIM), ((0, 0), (0, SUBLANES - 1), (0, 0)))
    b1 = jnp.stack([_mm([pos_rows[k:k + 1]], bf(w_cmp1[0, k]), tm=SUBLANES, bias=b_cmp1[0, k][None],
                        name="cmp_bias")[0, 0:1] for k in range(2)])
    slopes_a = _alibi(H_A)
    slopes_b = _alibi(H_B)
    gn = lambda v: v.reshape(1, D)
    peer_w = [(bf(w_pq[i].T), bf(peer_keys[i].reshape(2 * PEER_HEADS, N_KEYS, N_KEYS)), bf(peer_u[i]), bf(peer_v[i]))
              for i in range(2)]

    def rows_of(v, G, R, tq=TQ):
        return jnp.repeat(v.reshape(G, R), tq, axis=1)[..., None]

    z = _mm([x_prompt], w_in, tm=512, norm=(gn(g_norm[0, 0]), mp0[1], mp0[0]), name="nsa_in_p")
    kv_c, kv_s, kv_w = (z[..., NSA_Q + i * NSA_KV:NSA_Q + (i + 1) * NSA_KV] for i in range(3))
    q_t = bf(_heads_first(z[..., :NSA_Q] * (HEAD_DIM ** -0.5), G_A, R_A))
    glog = z[..., NSA_Q + 3 * NSA_KV:NSA_IN].reshape(B, S, 3, G_A, R_A).transpose(0, 3, 1, 2, 4)
    gates_t = jnp.pad(glog.reshape(B, G_A, S, 3 * R_A), ((0, 0), (0, 0), (0, 0), (0, 16 - 3 * R_A)))

    kvcT = _cmp_prompt(kv_c, w1p, b1, w2T)
    nb_p = S // CMP_STRIDE - 1
    ns_p = S // SEL_BLOCK
    ov_p = _overlap_matrix(kvcT.shape[-1], LANES, nb_p, ns_p)
    o_c, bits = _cmp_attend_prompt(q_t, kvcT, rows_of(slopes_a, G_A, R_A, TQ_CMP), gates_t, ov_p,
                                   tq=TQ_CMP, nb=nb_p, ns=ns_p)

    kpos = jnp.arange(S)
    kp_hi, kp_lo = ((kpos >> 6) * SEL_BLOCK).astype(F32), (kpos & (SEL_BLOCK - 1)).astype(F32)
    feat_k = bf(jnp.pad(jnp.stack([kp_hi, kp_lo, kp_hi, kp_lo], axis=-1), ((0, 0), (0, _QK_FEAT - 4))))

    def split_slopes(slopes, G, R):
        hi = bf(slopes)
        lo = bf(slopes - hi.astype(F32))
        rows = jnp.pad(jnp.stack([hi, hi, lo, lo], axis=-1), ((0, 0), (0, _QK_FEAT - 4)))
        eff = hi.astype(F32) + lo.astype(F32)
        return rows.reshape(G, R, _QK_FEAT), jnp.broadcast_to(eff.reshape(G, R, 1, 1), (G, R, 1, TQ))

    def flash_q(qz, feat_q, G, R):
        qT = bf(qz.reshape(B, S, G, R, HEAD_DIM) * (HEAD_DIM ** -0.5)).transpose(0, 2, 3, 4, 1)
        return jnp.concatenate([qT, jnp.broadcast_to(feat_q[None, :, :, :, None], (B, G, R, _QK_FEAT, S))], axis=3)

    def flash_kv(kv, G):
        kv5 = kv.reshape(B, S, 2, G, HEAD_DIM)
        k = jnp.concatenate([bf(kv5[:, :, 0].transpose(0, 2, 1, 3)),
                             jnp.broadcast_to(feat_k[None, None], (B, G, S, _QK_FEAT))], axis=-1)
        return k, bf(kv5[:, :, 1].transpose(0, 2, 3, 1))

    def tokens_first(oT):
        return oT.transpose(0, 4, 1, 2, 3).reshape(B, S, -1)

    feat_qa, _ = split_slopes(slopes_a, G_A, R_A)
    qT_a = flash_q(z[..., :NSA_Q], feat_qa, G_A, R_A)
    gates_tt = jnp.pad(glog.transpose(0, 1, 3, 4, 2).reshape(B, G_A, 3 * R_A, S), ((0, 0), (0, 0), (0, 16 - 3 * R_A), (0, 0)))
    k_s, vT_s = flash_kv(kv_s, G_A)
    o_s = _flash(qT_a, k_s, vT_s, tq=TQ, tk=512, bits=bits, gates_t=gates_tt, gate_row=1, name="sel_p")
    k_w, vT_w = flash_kv(kv_w, G_A)
    o_w = _flash(qT_a, k_w, vT_w, tq=TQ, tk=256, window=WIN_A, gates_t=gates_tt, gate_row=2, name="win_p")
    x1 = _mm([_heads_last(o_c), tokens_first(o_s), tokens_first(o_w)], bf(w_o_a[0]), tm=512,
             res=(x_prompt, mp0[2]), name="nsa_out_p")
    x1 = _peer(x1, mp0[4], mp0[3], mp0[5], gn(g_norm[0, 1]), *peer_w[0], t_tile=512, ei=8)

    kv_sh = _mm([x1], bf(w_kv_b), tm=512, norm=(gn(g_kv), mpk[1], mpk[0]), name="kv_sh_p")
    q1 = _mm([x1], bf(w_q_b[0]), tm=512, norm=(gn(g_norm[1, 0]), mp1[1], mp1[0]), name="q_b_p")
    feat_qb, slope_tb = split_slopes(slopes_b, G_B, R_B)
    k_b, vT_b = flash_kv(kv_sh, G_B)
    sink_tb = jnp.broadcast_to(sinks[0].reshape(G_B, R_B, 1, 1), (G_B, R_B, 1, TQ))
    o_b = _flash(flash_q(q1, feat_qb, G_B, R_B), k_b, vT_b, tq=TQ, tk=256, window=WIN_B,
                 sink_t=sink_tb, slope_t=slope_tb, name="swa_p")
    x2 = _mm([tokens_first(o_b)], bf(w_o_b[0]), tm=512, res=(x1, mp1[2]), name="swa_out_p")
    x2 = _peer(x2, mp1[4], mp1[3], mp1[5], gn(g_norm[1, 1]), *peer_w[1], t_tile=512, ei=8)
    y_prompt = _rmsnorm(x2, gn(g_final), tm=512)

    xs = x_sample.reshape(1, Bd, D)
    zs = _mm([xs], w_in, tm=Bd, norm=(gn(g_norm[0, 0]), ms0[1], ms0[0]), name="nsa_in_s")[0]
    kvs_new = [zs[:, NSA_Q + i * NSA_KV:NSA_Q + (i + 1) * NSA_KV] for i in range(3)]
    q16 = bf(zs[:, :NSA_Q] * (HEAD_DIM ** -0.5)).reshape(Bd, H_A, HEAD_DIM)
    gates16 = jnp.pad(zs[:, NSA_Q + 3 * NSA_KV:NSA_IN].reshape(Bd, 3, H_A).transpose(0, 2, 1), ((0, 0), (0, 0), (0, 1)))
    slope16_a = slopes_a.reshape(H_A, 1)

    cmpT = cache_cmp_kv[0].transpose(0, 2, 3, 4, 1)
    n_phys = cmpT.shape[0]
    kvc_new = jnp.pad(kvs_new[0][:, None, :], ((0, 0), (0, SUBLANES - 1), (0, 0)))
    kvcT_s = _cmp_sample(cmpT.reshape(n_phys, 2, 2, 2 * HEAD_DIM, PAGE_SIZE), page_table, kvc_new, w1p, b1, w2T)
    nb_s = kvcT_s.shape[-1] - 1
    ns_pad = 2 * LANES
    ov_s = _overlap_matrix(kvcT_s.shape[-1], ns_pad, nb_s + 1, n_past_blocks + 1)
    o_cs, idx_tile = _dec_cmp(q16, kvcT_s, slope16_a, gates16, ov_s, qpos=past_len, n_past_blocks=n_past_blocks)
    idx_flat = idx_tile[:, :G_A, :N_SEL].reshape(-1)

    selT = cache_sel_kv[0].transpose(0, 2, 3, 4, 1)
    kvs_new5 = kvs_new[1].reshape(Bd, 2, G_A, 1, HEAD_DIM)
    o_ss = _dec_sel(idx_flat, page_table, selT, q16, kvs_new5, slope16_a, gates16,
                    qpos=past_len, n_past_blocks=n_past_blocks)

    winT = state_win_kv[0].transpose(0, 2, 3, 4, 1)
    o_ws = _dec_slab(q16, winT, kvs_new[2].reshape(Bd, 2, G_A, HEAD_DIM), slope16_a, window=WIN_A, qpos=past_len,
                     gates16=gates16, gate_col=2, name="dec_win")
    flat = lambda o: o.reshape(1, Bd, H_A * HEAD_DIM)
    xs1 = _mm([flat(o_cs), flat(o_ss), flat(o_ws)], bf(w_o_a[0]), tm=Bd, res=(xs, ms0[2]), name="nsa_out_s")
    xs1 = _peer(xs1, ms0[4], ms0[3], ms0[5], gn(g_norm[0, 1]), *peer_w[0], t_tile=Bd, ei=8)

    kv_sh_s = _mm([xs1], bf(w_kv_b), tm=Bd, norm=(gn(g_kv), msk[1], msk[0]), name="kv_sh_s")[0]
    q1s = _mm([xs1], bf(w_q_b[0]), tm=Bd, norm=(gn(g_norm[1, 0]), ms1[1], ms1[0]), name="q_b_s")[0]
    q16b = bf(q1s * (HEAD_DIM ** -0.5)).reshape(Bd, H_B, HEAD_DIM)
    shT = state_shared_kv.transpose(0, 2, 3, 4, 1)
    o_bs = _dec_slab(q16b, shT, kv_sh_s.reshape(Bd, 2, G_B, HEAD_DIM), slopes_b.reshape(H_B, 1), window=WIN_B,
                     qpos=past_len, sink16=sinks[0].reshape(H_B, 1), name="dec_swa")
    xs2 = _mm([flat(o_bs)], bf(w_o_b[0]), tm=Bd, res=(xs1, ms1[2]), name="swa_out_s")
    xs2 = _peer(xs2, ms1[4], ms1[3], ms1[5], gn(g_norm[1, 1]), *peer_w[1], t_tile=Bd, ei=8)
    y_sample = _rmsnorm(xs2, gn(g_final), tm=Bd).reshape(Bd, 1, D)

    as_kv = lambda a, n, G: a.reshape(n, -1, 2, G, HEAD_DIM)
    nw = min(WIN_A, S)
    nwb = min(WIN_B, S)
    new_cmp_prompt = as_kv(kv_c, B, G_A)[None]
    new_sel_prompt = as_kv(kv_s, B, G_A)[None]
    new_win_prompt = as_kv(kv_w, B, G_A)[None, :, S - nw:]
    new_cmp_sample = as_kv(kvs_new[0], Bd, G_A)[None]
    new_sel_sample = as_kv(kvs_new[1], Bd, G_A)[None]
    new_win_sample = jnp.concatenate([state_win_kv[0], as_kv(kvs_new[2], Bd, G_A)], axis=1)[None, :, 1:]
    new_shared_prompt = as_kv(kv_sh, B, G_B)[:, S - nwb:]
    new_shared_sample = jnp.concatenate([state_shared_kv, as_kv(kv_sh_s, Bd, G_B)], axis=1)[:, 1:]
    return (y_prompt, y_sample, new_cmp_prompt, new_cmp_sample, new_sel_prompt, new_sel_sample,
            new_win_prompt, new_win_sample, new_shared_prompt, new_shared_sample)
```
